```python
import jax, jax.numpy as jnp
from jax import lax
import numpy as np

D_MODEL = 1024
BATCH = 32
SEQ = 2048
DEPTH = 4
DEC_BATCH = 16
DEC_SEQ = 2048
PAST_LEN = 128

GRID_W = 64
Q_BLOCK = 128
ROPE_THETA = 10000.0
EPS = 1e-6

A_HEADS = 8
A_KV_HEADS = 2
A_GROUP = A_HEADS // A_KV_HEADS
A_HEAD_DIM = D_MODEL // 16
A_Q_W = A_HEADS * A_HEAD_DIM
A_KV_W = A_KV_HEADS * A_HEAD_DIM

B_HEADS = 8
B_Q_RANK = 3 * D_MODEL // 8
B_KV_RANK = D_MODEL // 4
B_NOPE = 64
B_ROPE = 32
B_V = 64
B_QK = B_NOPE + B_ROPE

ATTN_IN_W = A_Q_W + 2 * A_KV_W + B_Q_RANK + B_KV_RANK + B_ROPE
ATTN_OUT_IN = A_HEADS * A_HEAD_DIM + B_HEADS * B_V

C_WIDTH = D_MODEL
C_GROUPS = 8
C_GROUP_W = C_WIDTH // C_GROUPS
C_CHUNK = 128

N_EXPERTS = 16
EXPERT_FF = 2 * D_MODEL
EC_CAPACITY_FACTOR = 2

N_ATTN_LAYERS = (DEPTH + 1) // 2
N_SGU_LAYERS = DEPTH // 2

kernel_name = "hybrid_gqa_mla_sgu_ec_encoder"


def rmsnorm(x, g):
    xf = x.astype(jnp.float32)
    y = xf * lax.rsqrt(jnp.mean(xf * xf, axis=-1, keepdims=True) + EPS)
    return (y * g.astype(jnp.float32)).astype(x.dtype)


def layernorm(x, g, b):
    xf = x.astype(jnp.float32)
    mu = jnp.mean(xf, axis=-1, keepdims=True)
    xc = xf - mu
    y = xc * lax.rsqrt(jnp.mean(xc * xc, axis=-1, keepdims=True) + EPS)
    return (y * g.astype(jnp.float32) + b.astype(jnp.float32)).astype(x.dtype)


def axial_rope_tables(seq_len, rot_dim):
    rows = seq_len // GRID_W
    row = jnp.repeat(jnp.arange(rows, dtype=jnp.float32), GRID_W)
    col = jnp.tile(jnp.arange(GRID_W, dtype=jnp.float32), rows)
    n_pair = rot_dim // 4
    freq = ROPE_THETA ** (-jnp.arange(n_pair, dtype=jnp.float32) / n_pair)
    ang = jnp.concatenate([row[:, None] * freq, col[:, None] * freq], axis=-1)
    return jnp.cos(ang), jnp.sin(ang)


def apply_rope(x, cos, sin):
    half = x.shape[-1] // 2
    xf = x.astype(jnp.float32)
    x1, x2 = xf[..., :half], xf[..., half:]
    c, s = cos[:, None, :], sin[:, None, :]
    return jnp.concatenate([x1 * c - x2 * s, x2 * c + x1 * s], axis=-1).astype(x.dtype)


def block_attention(q, k, v, scale):
    B, S, Hk, G, dq = q.shape
    nb = S // Q_BLOCK
    qb = q.reshape(B, nb, Q_BLOCK, Hk, G, dq).transpose(1, 0, 2, 3, 4, 5)

    def one_block(qi):
        s = jnp.einsum('bqhgd,bkhd->bhgqk', qi, k).astype(jnp.float32) * scale
        p = jax.nn.softmax(s, axis=-1).astype(v.dtype)
        return jnp.einsum('bhgqk,bkhd->bqhgd', p, v)

    o = lax.map(one_block, qb)
    return o.transpose(1, 0, 2, 3, 4, 5).reshape(B, S, Hk * G, v.shape[-1])


def attn_mixer(h, w_in, qn_g, kn_g, mq_g, w_uq, mkv_g, w_ukv, w_out, ropes):
    cos_a, sin_a, cos_b, sin_b = ropes
    B, S, _ = h.shape
    z = h @ w_in
    o0 = 0
    qa = z[..., o0:o0 + A_Q_W]; o0 += A_Q_W
    ka = z[..., o0:o0 + A_KV_W]; o0 += A_KV_W
    va = z[..., o0:o0 + A_KV_W]; o0 += A_KV_W
    cq = z[..., o0:o0 + B_Q_RANK]; o0 += B_Q_RANK
    ckv = z[..., o0:o0 + B_KV_RANK]; o0 += B_KV_RANK
    kr = z[..., o0:o0 + B_ROPE]

    qa = apply_rope(rmsnorm(qa.reshape(B, S, A_HEADS, A_HEAD_DIM), qn_g), cos_a, sin_a)
    ka = apply_rope(rmsnorm(ka.reshape(B, S, A_KV_HEADS, A_HEAD_DIM), kn_g), cos_a, sin_a)
    va = va.reshape(B, S, A_KV_HEADS, A_HEAD_DIM)
    oa = block_attention(qa.reshape(B, S, A_KV_HEADS, A_GROUP, A_HEAD_DIM), ka, va,
                         A_HEAD_DIM ** -0.5)

    qb = (rmsnorm(cq, mq_g) @ w_uq).reshape(B, S, B_HEADS, B_QK)
    q_rope = apply_rope(qb[..., B_NOPE:], cos_b, sin_b)
    kvb = (rmsnorm(ckv, mkv_g) @ w_ukv).reshape(B, S, B_HEADS, B_NOPE + B_V)
    k_nope, vb = kvb[..., :B_NOPE], kvb[..., B_NOPE:]
    k_rope = apply_rope(kr.reshape(B, S, 1, B_ROPE), cos_b, sin_b)
    k_full = jnp.concatenate([k_nope, jnp.broadcast_to(k_rope, (B, S, B_HEADS, B_ROPE))], axis=-1)
    q_full = jnp.concatenate([qb[..., :B_NOPE], q_rope], axis=-1)[:, :, :, None, :]
    ob = block_attention(q_full, k_full, vb, B_QK ** -0.5)

    o = jnp.concatenate([oa.reshape(B, S, -1), ob.reshape(B, S, -1)], axis=-1)
    return o @ w_out


def sgu_mixer(h, w_in, ln_g, ln_b, w_s, b_s, w_out):
    B, S, _ = h.shape
    z = jax.nn.gelu(h @ w_in)
    u, v = z[..., :C_WIDTH], z[..., C_WIDTH:]
    v = layernorm(v, ln_g, ln_b)
    vc = v.reshape(B, S // C_CHUNK, C_CHUNK, C_GROUPS, C_GROUP_W)
    s = jnp.einsum('gpq,bnqgc->bnpgc', w_s, vc) + b_s.T[None, None, :, :, None]
    return (u * s.reshape(B, S, C_WIDTH)) @ w_out


def ec_ffn(h, w_r, w_g, w_u, w_d):
    B, S, D = h.shape
    n_tok = B * S
    cap = EC_CAPACITY_FACTOR * n_tok // N_EXPERTS
    xf = h.reshape(n_tok, D)
    aff = jax.nn.softmax((xf @ w_r).astype(jnp.float32), axis=-1)
    gate, idx = lax.top_k(aff.T, cap)
    xg = xf[idx]
    hid = jax.nn.silu(jnp.einsum('ecd,edf->ecf', xg, w_g)) * jnp.einsum('ecd,edf->ecf', xg, w_u)
    y = jnp.einsum('ecf,efd->ecd', hid, w_d) * gate[..., None].astype(h.dtype)
    out = jnp.zeros_like(xf).at[idx.reshape(-1)].add(y.reshape(-1, D))
    return out.reshape(B, S, D)


def run_trunk(x, attn_norm, w_in_attn, qk_norm_q, qk_norm_k, mla_q_norm, w_uq, mla_kv_norm, w_ukv,
              w_out_attn, sgu_norm, w_in_sgu, sgu_ln_g, sgu_ln_b, w_spatial, b_spatial, w_out_sgu,
              ffn_norm, w_router, w_gate, w_up, w_down, final_norm):
    S = x.shape[1]
    cos_a, sin_a = axial_rope_tables(S, A_HEAD_DIM)
    cos_b, sin_b = axial_rope_tables(S, B_ROPE)
    ropes = (cos_a, sin_a, cos_b, sin_b)
    for l in range(DEPTH):
        i = l // 2
        if l % 2 == 0:
            x = x + attn_mixer(rmsnorm(x, attn_norm[i]), w_in_attn[i], qk_norm_q[i], qk_norm_k[i],
                               mla_q_norm[i], w_uq[i], mla_kv_norm[i], w_ukv[i], w_out_attn[i], ropes)
        else:
            x = x + sgu_mixer(rmsnorm(x, sgu_norm[i]), w_in_sgu[i], sgu_ln_g[i], sgu_ln_b[i],
                              w_spatial[i], b_spatial[i], w_out_sgu[i])
        x = x + ec_ffn(rmsnorm(x, ffn_norm[l]), w_router[l], w_gate[l], w_up[l], w_down[l])
    return rmsnorm(x, final_norm)


def setup_inputs(seed: int = 0) -> dict:
    key = jax.random.key(seed)
    ks = jax.random.split(key, 26)
    f32 = jnp.float32

    def nrm(k, shape, scale):
        return jax.random.normal(k, shape, f32) * scale

    def gain(k, shape):
        return 1.0 + 0.05 * jax.random.normal(k, shape, f32)

    NA, NC = N_ATTN_LAYERS, N_SGU_LAYERS
    return {
        "x_prompt": nrm(ks[0], (BATCH, SEQ, D_MODEL), 1.0),
        "x_sample": nrm(ks[1], (DEC_BATCH, DEC_SEQ, D_MODEL), 1.0),
        "attn_norm": gain(ks[2], (NA, D_MODEL)),
        "w_in_attn": nrm(ks[3], (NA, D_MODEL, ATTN_IN_W), D_MODEL ** -0.5),
        "qk_norm_q": gain(ks[4], (NA, A_HEAD_DIM)),
        "qk_norm_k": gain(ks[5], (NA, A_HEAD_DIM)),
        "mla_q_norm": gain(ks[6], (NA, B_Q_RANK)),
        "w_uq": nrm(ks[7], (NA, B_Q_RANK, B_HEADS * B_QK), B_Q_RANK ** -0.5),
        "mla_kv_norm": gain(ks[8], (NA, B_KV_RANK)),
        "w_ukv": nrm(ks[9], (NA, B_KV_RANK, B_HEADS * (B_NOPE + B_V)), B_KV_RANK ** -0.5),
        "w_out_attn": nrm(ks[10], (NA, ATTN_OUT_IN, D_MODEL), ATTN_OUT_IN ** -0.5),
        "sgu_norm": gain(ks[11], (NC, D_MODEL)),
        "w_in_sgu": nrm(ks[12], (NC, D_MODEL, 2 * C_WIDTH), D_MODEL ** -0.5),
        "sgu_ln_g": gain(ks[13], (NC, C_WIDTH)),
        "sgu_ln_b": nrm(ks[14], (NC, C_WIDTH), 0.02),
        "w_spatial": nrm(ks[15], (NC, C_GROUPS, C_CHUNK, C_CHUNK), C_CHUNK ** -0.5),
        "b_spatial": 1.0 + nrm(ks[16], (NC, C_GROUPS, C_CHUNK), 0.1),
        "w_out_sgu": nrm(ks[17], (NC, C_WIDTH, D_MODEL), C_WIDTH ** -0.5),
        "ffn_norm": gain(ks[18], (DEPTH, D_MODEL)),
        "w_router": nrm(ks[19], (DEPTH, D_MODEL, N_EXPERTS), D_MODEL ** -0.5),
        "w_gate": nrm(ks[20], (DEPTH, N_EXPERTS, D_MODEL, EXPERT_FF), D_MODEL ** -0.5),
        "w_up": nrm(ks[21], (DEPTH, N_EXPERTS, D_MODEL, EXPERT_FF), D_MODEL ** -0.5),
        "w_down": nrm(ks[22], (DEPTH, N_EXPERTS, EXPERT_FF, D_MODEL), EXPERT_FF ** -0.5),
        "final_norm": gain(ks[23], (D_MODEL,)),
    }


def reference(x_prompt, x_sample, attn_norm, w_in_attn, qk_norm_q, qk_norm_k, mla_q_norm, w_uq,
              mla_kv_norm, w_ukv, w_out_attn, sgu_norm, w_in_sgu, sgu_ln_g, sgu_ln_b, w_spatial,
              b_spatial, w_out_sgu, ffn_norm, w_router, w_gate, w_up, w_down, final_norm):
    y_prompt = run_trunk(x_prompt, attn_norm, w_in_attn, qk_norm_q, qk_norm_k, mla_q_norm, w_uq,
                         mla_kv_norm, w_ukv, w_out_attn, sgu_norm, w_in_sgu, sgu_ln_g, sgu_ln_b,
                         w_spatial, b_spatial, w_out_sgu, ffn_norm, w_router, w_gate, w_up, w_down,
                         final_norm)
    y_sample = run_trunk(x_sample, attn_norm, w_in_attn, qk_norm_q, qk_norm_k, mla_q_norm, w_uq,
                         mla_kv_norm, w_ukv, w_out_attn, sgu_norm, w_in_sgu, sgu_ln_g, sgu_ln_b,
                         w_spatial, b_spatial, w_out_sgu, ffn_norm, w_router, w_gate, w_up, w_down,
                         final_norm)
    return (y_prompt, y_sample)
```

```python
import functools

import jax
import jax.numpy as jnp
from jax import lax
from jax.experimental import pallas as pl
from jax.experimental.pallas import tpu as pltpu

D_MODEL = 1024
GRID_W = 64
ROPE_THETA = 10000.0
EPS = 1e-6

A_HEADS = 8
A_KV_HEADS = 2
A_GROUP = A_HEADS // A_KV_HEADS
A_HEAD_DIM = 64
A_Q_W = A_HEADS * A_HEAD_DIM
A_KV_W = A_KV_HEADS * A_HEAD_DIM

B_HEADS = 8
B_Q_RANK = 384
B_KV_RANK = 256
B_NOPE = 64
B_ROPE = 32
B_V = 64
B_QK = B_NOPE + B_ROPE

N_HEADS = A_HEADS + B_HEADS
N_KV = A_KV_HEADS + B_HEADS

C_WIDTH = D_MODEL
C_GROUPS = 8
C_GROUP_W = C_WIDTH // C_GROUPS
C_CHUNK = 128

N_EXPERTS = 16
EXPERT_FF = 2 * D_MODEL
EC_CAPACITY_FACTOR = 2

LANE = 128
VMEM_LIMIT = 56 * 1024 * 1024
MAX_PREFETCH_IDX = 65536

BF16 = jnp.bfloat16
F32 = jnp.float32


def _params(*sem):
    return pltpu.CompilerParams(dimension_semantics=sem, vmem_limit_bytes=VMEM_LIMIT)


def _dot(a, b):
    return jnp.dot(a.astype(BF16), b.astype(BF16), preferred_element_type=F32)


def _dot_nt(a, b):
    return lax.dot_general(a.astype(BF16), b.astype(BF16), (((1,), (1,)), ((), ())),
                           preferred_element_type=F32)


def _rms(x, g):
    return x * lax.rsqrt(jnp.mean(x * x, axis=-1, keepdims=True) + EPS) * g


def _router_tail(x1, g_ref, wrt_ref, xn_ref, aff_ref):
    xn = _rms(x1, g_ref[...])
    xn_ref[...] = xn
    logits = _dot_nt(wrt_ref[...], xn)
    m = jnp.max(logits, axis=0, keepdims=True)
    e = jnp.exp(logits - m)
    aff_ref[...] = e / jnp.sum(e, axis=0, keepdims=True)


def _head_norm(z, n_heads, width):
    outs = []
    for h in range(n_heads):
        blk = z[:, h * LANE:(h + 1) * LANE]
        ms = jnp.sum(blk * blk, axis=-1, keepdims=True) * (1.0 / width)
        outs.append(blk * lax.rsqrt(ms + EPS))
    return jnp.concatenate(outs, axis=1)


def _rope(x, c, s_lo, s_hi, half, n_heads):
    w = x.shape[1]
    c = jnp.tile(c, (1, n_heads))
    s_lo = jnp.tile(s_lo, (1, n_heads))
    s_hi = jnp.tile(s_hi, (1, n_heads))
    return x * c + pltpu.roll(x, w - half, 1) * s_lo + pltpu.roll(x, half, 1) * s_hi


def _attn_proj_kernel(x_ref, g_ref, w_ref, gq_ref, gk_ref, mq_ref, wuq_ref, mkv_ref, wukv_ref,
                      ca_ref, sa1_ref, sa2_ref, cb_ref, sb1_ref, sb2_ref,
                      q_ref, k_ref, v_ref):
    h = _rms(x_ref[...], g_ref[...])
    z = _dot(h, w_ref[...])
    o = 0
    qa = z[:, o:o + A_HEADS * LANE]; o += A_HEADS * LANE
    ka = z[:, o:o + A_KV_HEADS * LANE]; o += A_KV_HEADS * LANE
    va = z[:, o:o + A_KV_HEADS * LANE]; o += A_KV_HEADS * LANE
    cq = z[:, o:o + B_Q_RANK]; o += B_Q_RANK
    ckv = z[:, o:o + B_KV_RANK]; o += B_KV_RANK
    kr = z[:, o:o + LANE]

    ca, sa1, sa2 = ca_ref[...], sa1_ref[...], sa2_ref[...]
    cb, sb1, sb2 = cb_ref[...], sb1_ref[...], sb2_ref[...]

    qa = _rope(_head_norm(qa, A_HEADS, A_HEAD_DIM) * gq_ref[...], ca, sa1, sa2, A_HEAD_DIM // 2, A_HEADS)
    ka = _rope(_head_norm(ka, A_KV_HEADS, A_HEAD_DIM) * gk_ref[...], ca, sa1, sa2, A_HEAD_DIM // 2,
               A_KV_HEADS)

    qb = _rope(_dot(_rms(cq, mq_ref[...]), wuq_ref[...]), cb, sb1, sb2, B_ROPE // 2, B_HEADS)
    kvb = _dot(_rms(ckv, mkv_ref[...]), wukv_ref[...])
    kr = _rope(kr, cb, sb1, sb2, B_ROPE // 2, 1)
    kb = kvb[:, :B_HEADS * LANE] + jnp.tile(kr, (1, B_HEADS))
    vb = kvb[:, B_HEADS * LANE:]

    q_ref[...] = jnp.concatenate([qa, qb], axis=1).astype(BF16)
    k_ref[...] = jnp.concatenate([ka, kb], axis=1).astype(BF16)
    v_ref[...] = jnp.concatenate([va, vb], axis=1).astype(BF16)


def _attn_proj(x, lw, seq, tile):
    n = x.shape[0]
    nt = n // tile
    per_seq = seq // tile
    full = lambda a: pl.BlockSpec(a.shape, lambda i: (0,) * a.ndim)
    tok = lambda w: pl.BlockSpec((tile, w), lambda i: (i, 0))
    tab = pl.BlockSpec((tile, LANE), lambda i: (i % per_seq, 0))
    weights = (lw["g"], lw["w_all"], lw["gq"], lw["gk"], lw["mq"], lw["wuq"], lw["mkv"], lw["wukv"])
    return pl.pallas_call(
        _attn_proj_kernel,
        grid=(nt,),
        in_specs=[tok(D_MODEL)] + [full(a) for a in weights] + [tab] * 6,
        out_specs=[tok(N_HEADS * LANE), tok(N_KV * LANE), tok(N_KV * LANE)],
        out_shape=[jax.ShapeDtypeStruct((n, N_HEADS * LANE), BF16),
                   jax.ShapeDtypeStruct((n, N_KV * LANE), BF16),
                   jax.ShapeDtypeStruct((n, N_KV * LANE), BF16)],
        compiler_params=_params("parallel"),
        name="attn_proj",
    )(x, *weights, *lw["rope"])


def _attn_kernel(q_ref, k_ref, v_ref, o_ref):
    head = pl.program_id(1)
    scale = jnp.where(head < A_HEADS, A_HEAD_DIM ** -0.5, B_QK ** -0.5).astype(F32)
    s = _dot_nt(q_ref[...], k_ref[...]) * scale
    m = jnp.max(s, axis=-1, keepdims=True)
    p = jnp.exp(s - m)
    l = jnp.sum(p, axis=-1, keepdims=True)
    o = _dot(p, v_ref[...])
    o_ref[...] = (o / l).astype(BF16)


def _kv_index(h):
    return jnp.where(h < A_HEADS, h // A_GROUP, h - (A_HEADS - A_KV_HEADS))


def _attention(q, k, v, batch, seq, tq):
    n = q.shape[0]
    nq = seq // tq
    return pl.pallas_call(
        _attn_kernel,
        grid=(batch, N_HEADS, nq),
        in_specs=[pl.BlockSpec((tq, LANE), lambda b, h, i: (b * nq + i, h)),
                  pl.BlockSpec((seq, LANE), lambda b, h, i: (b, _kv_index(h))),
                  pl.BlockSpec((seq, LANE), lambda b, h, i: (b, _kv_index(h)))],
        out_specs=pl.BlockSpec((tq, LANE), lambda b, h, i: (b * nq + i, h)),
        out_shape=jax.ShapeDtypeStruct((n, N_HEADS * LANE), BF16),
        compiler_params=_params("parallel", "parallel", "parallel"),
        name="attention",
    )(q, k, v)


def _attn_out_kernel(o_ref, x_ref, w_ref, g_ref, wrt_ref, x1_ref, xn_ref, aff_ref):
    x1 = x_ref[...] + _dot(o_ref[...], w_ref[...])
    x1_ref[...] = x1
    _router_tail(x1, g_ref, wrt_ref, xn_ref, aff_ref)


def _tail_specs(n, tile):
    out_specs = [pl.BlockSpec((tile, D_MODEL), lambda i: (i, 0)),
                 pl.BlockSpec((tile, D_MODEL), lambda i: (i, 0)),
                 pl.BlockSpec((N_EXPERTS, tile), lambda i: (0, i))]
    out_shape = [jax.ShapeDtypeStruct((n, D_MODEL), F32),
                 jax.ShapeDtypeStruct((n, D_MODEL), F32),
                 jax.ShapeDtypeStruct((N_EXPERTS, n), F32)]
    return out_specs, out_shape


def _attn_out(o, x, lw, fw, tile):
    n = x.shape[0]
    full = lambda a: pl.BlockSpec(a.shape, lambda i: (0,) * a.ndim)
    out_specs, out_shape = _tail_specs(n, tile)
    weights = (lw["w_out"], fw["g"], fw["wrt"])
    return pl.pallas_call(
        _attn_out_kernel,
        grid=(n // tile,),
        in_specs=[pl.BlockSpec((tile, N_HEADS * LANE), lambda i: (i, 0)),
                  pl.BlockSpec((tile, D_MODEL), lambda i: (i, 0))] + [full(a) for a in weights],
        out_specs=out_specs,
        out_shape=out_shape,
        compiler_params=_params("parallel"),
        name="attn_out",
    )(o, x, *weights)


def _sgu_kernel(x_ref, g_ref, win_ref, lng_ref, lnb_ref, ws_ref, bs_ref, wout_ref, gf_ref, wrt_ref,
                x1_ref, xn_ref, aff_ref):
    x = x_ref[...]
    tile = x.shape[0]
    z = jax.nn.gelu(_dot(_rms(x, g_ref[...]), win_ref[...]))
    u, v = z[:, :C_WIDTH], z[:, C_WIDTH:]
    mu = jnp.mean(v, axis=-1, keepdims=True)
    vc = v - mu
    v = vc * lax.rsqrt(jnp.mean(vc * vc, axis=-1, keepdims=True) + EPS) * lng_ref[...] + lnb_ref[...]
    v = v.astype(BF16)
    bias = bs_ref[...]
    rows = []
    for c in range(tile // C_CHUNK):
        vch = v[c * C_CHUNK:(c + 1) * C_CHUNK, :]
        cols = [_dot(ws_ref[g], vch[:, g * C_GROUP_W:(g + 1) * C_GROUP_W]) for g in range(C_GROUPS)]
        rows.append(jnp.concatenate(cols, axis=1) + bias)
    s = jnp.concatenate(rows, axis=0)
    x1 = x + _dot(u * s, wout_ref[...])
    x1_ref[...] = x1
    _router_tail(x1, gf_ref, wrt_ref, xn_ref, aff_ref)


def _sgu(x, lw, fw, tile):
    n = x.shape[0]
    full = lambda a: pl.BlockSpec(a.shape, lambda i: (0,) * a.ndim)
    out_specs, out_shape = _tail_specs(n, tile)
    weights = (lw["g"], lw["w_in"], lw["ln_g"], lw["ln_b"], lw["w_s"], lw["b_s"], lw["w_out"],
               fw["g"], fw["wrt"])
    return pl.pallas_call(
        _sgu_kernel,
        grid=(n // tile,),
        in_specs=[pl.BlockSpec((tile, D_MODEL), lambda i: (i, 0))] + [full(a) for a in weights],
        out_specs=out_specs,
        out_shape=out_shape,
        compiler_params=_params("parallel"),
        name="sgu",
    )(x, *weights)


def _ffn_kernel(idx_ref, gate_ref, xn_hbm, wg_ref, wu_ref, wd_ref, y_ref, buf, sem, *, rows, n_steps):
    step = pl.program_id(0) * pl.num_programs(1) + pl.program_id(1)
    slot = step % 2

    def row_copy(src_row, dst_slot, dst_row):
        return pltpu.make_async_copy(xn_hbm.at[pl.ds(src_row, 1), :],
                                     buf.at[dst_slot, pl.ds(dst_row, 1), :], sem.at[dst_slot])

    def gather(chunk, dst_slot):
        def body(r, carry):
            row_copy(idx_ref[chunk * rows + r], dst_slot, r).start()
            return carry
        lax.fori_loop(0, rows, body, 0)

    @pl.when(step == 0)
    def _():
        gather(0, 0)

    @pl.when(step + 1 < n_steps)
    def _():
        gather(step + 1, 1 - slot)

    pltpu.make_async_copy(xn_hbm.at[pl.ds(0, rows), :], buf.at[slot], sem.at[slot]).wait()

    xg = buf[slot].astype(BF16)
    hid = jax.nn.silu(_dot(xg, wg_ref[0])) * _dot(xg, wu_ref[0])
    y_ref[...] = _dot(hid, wd_ref[0]) * gate_ref[...]


def _ffn(xn, idx, gate, fw, rows):
    n_e, cap = idx.shape
    chunks = cap // rows
    e_call = max(1, min(n_e, MAX_PREFETCH_IDX // cap))
    outs = []
    for e0 in range(0, n_e, e_call):
        n_steps = e_call * chunks
        w_spec = lambda shape, e0=e0: pl.BlockSpec((1,) + shape, lambda e, c, idx: (e + e0, 0, 0))
        grid_spec = pltpu.PrefetchScalarGridSpec(
            num_scalar_prefetch=1,
            grid=(e_call, chunks),
            in_specs=[pl.BlockSpec((rows, 1), lambda e, c, idx: (e * chunks + c, 0)),
                      pl.BlockSpec(memory_space=pl.ANY),
                      w_spec((D_MODEL, EXPERT_FF)), w_spec((D_MODEL, EXPERT_FF)),
                      w_spec((EXPERT_FF, D_MODEL))],
            out_specs=pl.BlockSpec((rows, D_MODEL), lambda e, c, idx: (e * chunks + c, 0)),
            scratch_shapes=[pltpu.VMEM((2, rows, D_MODEL), F32), pltpu.SemaphoreType.DMA((2,))],
        )
        outs.append(pl.pallas_call(
            functools.partial(_ffn_kernel, rows=rows, n_steps=n_steps),
            grid_spec=grid_spec,
            out_shape=jax.ShapeDtypeStruct((e_call * cap, D_MODEL), F32),
            compiler_params=_params("arbitrary", "arbitrary"),
            name="ec_ffn",
        )(idx[e0:e0 + e_call].reshape(-1), gate[e0:e0 + e_call].reshape(-1, 1), xn,
          fw["w_gate"], fw["w_up"], fw["w_down"]))
    return jnp.concatenate(outs, axis=0) if len(outs) > 1 else outs[0]


def _ec_ffn(x1, xn, aff_t, fw, rows):
    n = x1.shape[0]
    cap = EC_CAPACITY_FACTOR * n // N_EXPERTS
    gate, idx = lax.top_k(aff_t, cap)
    y = _ffn(xn, idx, gate, fw, min(rows, cap))
    return x1.at[idx.reshape(-1)].add(y)


def _final_norm_kernel(x_ref, g_ref, o_ref):
    o_ref[...] = _rms(x_ref[...], g_ref[...])


def _final_norm(x, g, tile):
    n = x.shape[0]
    return pl.pallas_call(
        _final_norm_kernel,
        grid=(n // tile,),
        in_specs=[pl.BlockSpec((tile, D_MODEL), lambda i: (i, 0)),
                  pl.BlockSpec((1, D_MODEL), lambda i: (0, 0))],
        out_specs=pl.BlockSpec((tile, D_MODEL), lambda i: (i, 0)),
        out_shape=jax.ShapeDtypeStruct((n, D_MODEL), F32),
        compiler_params=_params("parallel"),
        name="final_norm",
    )(x, g)


def _pad_heads(w, n_heads, width):
    lead = w.shape[:-1]
    w = w.reshape(lead + (n_heads, width))
    w = jnp.pad(w, [(0, 0)] * len(lead) + [(0, 0), (0, LANE - width)])
    return w.reshape(lead + (n_heads * LANE,))


def _rope_tables(seq):
    rows = seq // GRID_W
    row = jnp.repeat(jnp.arange(rows, dtype=F32), GRID_W)
    col = jnp.tile(jnp.arange(GRID_W, dtype=F32), rows)

    def cos_sin(rot_dim):
        n_pair = rot_dim // 4
        freq = ROPE_THETA ** (-jnp.arange(n_pair, dtype=F32) / n_pair)
        ang = jnp.concatenate([row[:, None] * freq, col[:, None] * freq], axis=-1)
        return jnp.cos(ang), jnp.sin(ang)

    def place(parts):
        out = jnp.zeros((seq, LANE), F32)
        for start, val in parts:
            out = out.at[:, start:start + val.shape[1]].set(val)
        return out

    ca, sa = cos_sin(A_HEAD_DIM)
    cb, sb = cos_sin(B_ROPE)
    ha, hb = A_HEAD_DIM // 2, B_ROPE // 2
    ones = jnp.ones((seq, B_NOPE), F32)
    return (place([(0, ca), (ha, ca)]), place([(0, -sa)]), place([(ha, sa)]),
            place([(0, ones), (B_NOPE, cb), (B_NOPE + hb, cb)]), place([(B_NOPE, -sb)]),
            place([(B_NOPE + hb, sb)]))


def _prep_attn_layer(i, p, rope):
    w_in = p["w_in_attn"][i]
    o = 0
    w_qa = w_in[:, o:o + A_Q_W]; o += A_Q_W
    w_ka = w_in[:, o:o + A_KV_W]; o += A_KV_W
    w_va = w_in[:, o:o + A_KV_W]; o += A_KV_W
    w_cq = w_in[:, o:o + B_Q_RANK]; o += B_Q_RANK
    w_ckv = w_in[:, o:o + B_KV_RANK]; o += B_KV_RANK
    w_kr = w_in[:, o:o + B_ROPE]
    w_kr = jnp.pad(w_kr, [(0, 0), (B_NOPE, LANE - B_NOPE - B_ROPE)])
    w_all = jnp.concatenate([_pad_heads(w_qa, A_HEADS, A_HEAD_DIM), _pad_heads(w_ka, A_KV_HEADS, A_HEAD_DIM),
                             _pad_heads(w_va, A_KV_HEADS, A_HEAD_DIM), w_cq, w_ckv, w_kr], axis=1)
    w_ukv = p["w_ukv"][i].reshape(B_KV_RANK, B_HEADS, B_NOPE + B_V)
    w_uk = _pad_heads(w_ukv[:, :, :B_NOPE].reshape(B_KV_RANK, -1), B_HEADS, B_NOPE)
    w_uv = _pad_heads(w_ukv[:, :, B_NOPE:].reshape(B_KV_RANK, -1), B_HEADS, B_V)
    w_out = p["w_out_attn"][i].reshape(N_HEADS, A_HEAD_DIM, D_MODEL)
    w_out = jnp.pad(w_out, [(0, 0), (0, LANE - A_HEAD_DIM), (0, 0)]).reshape(N_HEADS * LANE, D_MODEL)
    return {
        "g": p["attn_norm"][i][None, :],
        "w_all": w_all.astype(BF16),
        "gq": jnp.tile(jnp.pad(p["qk_norm_q"][i], (0, LANE - A_HEAD_DIM)), A_HEADS)[None, :],
        "gk": jnp.tile(jnp.pad(p["qk_norm_k"][i], (0, LANE - A_HEAD_DIM)), A_KV_HEADS)[None, :],
        "mq": p["mla_q_norm"][i][None, :],
        "wuq": _pad_heads(p["w_uq"][i], B_HEADS, B_QK).astype(BF16),
        "mkv": p["mla_kv_norm"][i][None, :],
        "wukv": jnp.concatenate([w_uk, w_uv], axis=1).astype(BF16),
        "w_out": w_out.astype(BF16),
        "rope": rope,
    }


def _prep_sgu_layer(i, p):
    b_s = p["b_spatial"][i]
    return {
        "g": p["sgu_norm"][i][None, :],
        "w_in": p["w_in_sgu"][i].astype(BF16),
        "ln_g": p["sgu_ln_g"][i][None, :],
        "ln_b": p["sgu_ln_b"][i][None, :],
        "w_s": p["w_spatial"][i].astype(BF16),
        "b_s": jnp.repeat(b_s.T, C_GROUP_W, axis=1),
        "w_out": p["w_out_sgu"][i].astype(BF16),
    }


def _prep_ffn_layer(l, p, w_gate, w_up, w_down):
    return {
        "g": p["ffn_norm"][l][None, :],
        "wrt": p["w_router"][l].T.astype(BF16),
        "w_gate": w_gate[l], "w_up": w_up[l], "w_down": w_down[l],
    }


def _pick(n, target):
    t = min(n, target)
    while n % t:
        t //= 2
    return t


def _trunk(x, attn_layers, sgu_layers, ffn_layers, final_g):
    batch, seq, _ = x.shape
    n = batch * seq
    x = x.reshape(n, D_MODEL)
    tile = _pick(seq, 512)
    tq = _pick(seq, 1024)
    for l, fw in enumerate(ffn_layers):
        if l % 2 == 0:
            lw = attn_layers[l // 2]
            q, k, v = _attn_proj(x, lw, seq, tile)
            o = _attention(q, k, v, batch, seq, tq)
            x1, xn, aff_t = _attn_out(o, x, lw, fw, tile)
        else:
            x1, xn, aff_t = _sgu(x, sgu_layers[l // 2], fw, tile)
        x = _ec_ffn(x1, xn, aff_t, fw, 512)
    return _final_norm(x, final_g[None, :], tile).reshape(batch, seq, D_MODEL)


def kernel(x_prompt, x_sample, attn_norm, w_in_attn, qk_norm_q, qk_norm_k, mla_q_norm, w_uq, mla_kv_norm, w_ukv, w_out_attn, sgu_norm, w_in_sgu, sgu_ln_g, sgu_ln_b, w_spatial, b_spatial, w_out_sgu, ffn_norm, w_router, w_gate, w_up, w_down, final_norm):
    p = dict(attn_norm=attn_norm, w_in_attn=w_in_attn, qk_norm_q=qk_norm_q, qk_norm_k=qk_norm_k,
             mla_q_norm=mla_q_norm, w_uq=w_uq, mla_kv_norm=mla_kv_norm, w_ukv=w_ukv,
             w_out_attn=w_out_attn, sgu_norm=sgu_norm, w_in_sgu=w_in_sgu, sgu_ln_g=sgu_ln_g,
             sgu_ln_b=sgu_ln_b, w_spatial=w_spatial, b_spatial=b_spatial, w_out_sgu=w_out_sgu,
             ffn_norm=ffn_norm, w_router=w_router)
    depth = ffn_norm.shape[0]
    wg, wu, wd = w_gate.astype(BF16), w_up.astype(BF16), w_down.astype(BF16)
    ffn_layers = [_prep_ffn_layer(l, p, wg, wu, wd) for l in range(depth)]
    sgu_layers = [_prep_sgu_layer(i, p) for i in range(sgu_norm.shape[0])]
    outs = []
    for x in (x_prompt, x_sample):
        rope = _rope_tables(x.shape[1])
        attn_layers = [_prep_attn_layer(i, p, rope) for i in range(attn_norm.shape[0])]
        outs.append(_trunk(x, attn_layers, sgu_layers, ffn_layers, final_norm))
    return tuple(outs)
```

```python
import functools

import jax
import jax.numpy as jnp
from jax import lax
from jax.experimental import pallas as pl
from jax.experimental.pallas import tpu as pltpu

D_MODEL = 1024
GRID_W = 64
ROPE_THETA = 10000.0
EPS = 1e-6

A_HEADS = 8
A_KV_HEADS = 2
A_GROUP = A_HEADS // A_KV_HEADS
A_HEAD_DIM = 64
A_Q_W = A_HEADS * A_HEAD_DIM
A_KV_W = A_KV_HEADS * A_HEAD_DIM

B_HEADS = 8
B_Q_RANK = 384
B_KV_RANK = 256
B_NOPE = 64
B_ROPE = 32
B_V = 64
B_QK = B_NOPE + B_ROPE

N_HEADS = A_HEADS + B_HEADS
N_KV = A_KV_HEADS + B_HEADS

C_WIDTH = D_MODEL
C_GROUPS = 8
C_GROUP_W = C_WIDTH // C_GROUPS
C_CHUNK = 128

N_EXPERTS = 16
EXPERT_FF = 2 * D_MODEL
EC_CAPACITY_FACTOR = 2

LANE = 128
VMEM_LIMIT = 56 * 1024 * 1024
SEL_BLOCK = LANE
SEL_STEP = 512
FFN_ROWS = 512
COMBINE_W = 96
ROW_UNROLL = 8

BF16 = jnp.bfloat16
F32 = jnp.float32


def _params(*sem):
    return pltpu.CompilerParams(dimension_semantics=sem, vmem_limit_bytes=VMEM_LIMIT)


def _dot(a, b):
    return jnp.dot(a.astype(BF16), b.astype(BF16), preferred_element_type=F32)


def _dot_nt(a, b):
    return lax.dot_general(a.astype(BF16), b.astype(BF16), (((1,), (1,)), ((), ())),
                           preferred_element_type=F32)


def _rms(x, g):
    return x * lax.rsqrt(jnp.mean(x * x, axis=-1, keepdims=True) + EPS) * g


def _router_tail(x1, g_ref, wr_ref, wrt_ref, xn_ref, aff_ref, affn_ref):
    xn = _rms(x1, g_ref[...])
    xn_ref[...] = xn
    logits = _dot_nt(wrt_ref[...], xn)
    m = jnp.max(logits, axis=0, keepdims=True)
    e = jnp.exp(logits - m)
    aff_ref[...] = e / jnp.sum(e, axis=0, keepdims=True)
    logits = _dot(xn, wr_ref[...])
    m = jnp.max(logits, axis=1, keepdims=True)
    e = jnp.exp(logits - m)
    affn_ref[...] = e / jnp.sum(e, axis=1, keepdims=True)


def _head_norm(z, n_heads, width):
    outs = []
    for h in range(n_heads):
        blk = z[:, h * LANE:(h + 1) * LANE]
        ms = jnp.sum(blk * blk, axis=-1, keepdims=True) * (1.0 / width)
        outs.append(blk * lax.rsqrt(ms + EPS))
    return jnp.concatenate(outs, axis=1)


def _rope(x, c, s_lo, s_hi, half, n_heads):
    w = x.shape[1]
    c = jnp.tile(c, (1, n_heads))
    s_lo = jnp.tile(s_lo, (1, n_heads))
    s_hi = jnp.tile(s_hi, (1, n_heads))
    return x * c + pltpu.roll(x, w - half, 1) * s_lo + pltpu.roll(x, half, 1) * s_hi


def _attn_proj_kernel(x_ref, g_ref, w_ref, gq_ref, gk_ref, mq_ref, wuq_ref, mkv_ref, wukv_ref,
                      ca_ref, sa1_ref, sa2_ref, cb_ref, sb1_ref, sb2_ref,
                      q_ref, k_ref, v_ref):
    h = _rms(x_ref[...], g_ref[...])
    z = _dot(h, w_ref[...])
    o = 0
    qa = z[:, o:o + A_HEADS * LANE]; o += A_HEADS * LANE
    ka = z[:, o:o + A_KV_HEADS * LANE]; o += A_KV_HEADS * LANE
    va = z[:, o:o + A_KV_HEADS * LANE]; o += A_KV_HEADS * LANE
    cq = z[:, o:o + B_Q_RANK]; o += B_Q_RANK
    ckv = z[:, o:o + B_KV_RANK]; o += B_KV_RANK
    kr = z[:, o:o + LANE]

    ca, sa1, sa2 = ca_ref[...], sa1_ref[...], sa2_ref[...]
    cb, sb1, sb2 = cb_ref[...], sb1_ref[...], sb2_ref[...]

    qa = _rope(_head_norm(qa, A_HEADS, A_HEAD_DIM) * gq_ref[...], ca, sa1, sa2, A_HEAD_DIM // 2, A_HEADS)
    ka = _rope(_head_norm(ka, A_KV_HEADS, A_HEAD_DIM) * gk_ref[...], ca, sa1, sa2, A_HEAD_DIM // 2,
               A_KV_HEADS)

    qb = _rope(_dot(_rms(cq, mq_ref[...]), wuq_ref[...]), cb, sb1, sb2, B_ROPE // 2, B_HEADS)
    kvb = _dot(_rms(ckv, mkv_ref[...]), wukv_ref[...])
    kr = _rope(kr, cb, sb1, sb2, B_ROPE // 2, 1)
    kb = kvb[:, :B_HEADS * LANE] + jnp.tile(kr, (1, B_HEADS))
    vb = kvb[:, B_HEADS * LANE:]

    q_ref[...] = jnp.concatenate([qa, qb], axis=1).astype(BF16)
    k_ref[...] = jnp.concatenate([ka, kb], axis=1).astype(BF16)
    v_ref[...] = jnp.concatenate([va, vb], axis=1).astype(BF16)


def _attn_proj(x, lw, seq, tile):
    n = x.shape[0]
    nt = n // tile
    per_seq = seq // tile
    full = lambda a: pl.BlockSpec(a.shape, lambda i: (0,) * a.ndim)
    tok = lambda w: pl.BlockSpec((tile, w), lambda i: (i, 0))
    tab = pl.BlockSpec((tile, LANE), lambda i: (i % per_seq, 0))
    weights = (lw["g"], lw["w_all"], lw["gq"], lw["gk"], lw["mq"], lw["wuq"], lw["mkv"], lw["wukv"])
    return pl.pallas_call(
        _attn_proj_kernel,
        grid=(nt,),
        in_specs=[tok(D_MODEL)] + [full(a) for a in weights] + [tab] * 6,
        out_specs=[tok(N_HEADS * LANE), tok(N_KV * LANE), tok(N_KV * LANE)],
        out_shape=[jax.ShapeDtypeStruct((n, N_HEADS * LANE), BF16),
                   jax.ShapeDtypeStruct((n, N_KV * LANE), BF16),
                   jax.ShapeDtypeStruct((n, N_KV * LANE), BF16)],
        compiler_params=_params("parallel"),
        name="attn_proj",
    )(x, *weights, *lw["rope"])


def _attn_kernel(q_ref, k_ref, v_ref, o_ref):
    head = pl.program_id(1)
    scale = jnp.where(head < A_HEADS, A_HEAD_DIM ** -0.5, B_QK ** -0.5).astype(F32)
    s = _dot_nt(q_ref[...], k_ref[...]) * scale
    m = jnp.max(s, axis=-1, keepdims=True)
    p = jnp.exp(s - m)
    l = jnp.sum(p, axis=-1, keepdims=True)
    o = _dot(p, v_ref[...])
    o_ref[...] = (o / l).astype(BF16)


def _kv_index(h):
    return jnp.where(h < A_HEADS, h // A_GROUP, h - (A_HEADS - A_KV_HEADS))


def _attention(q, k, v, batch, seq, tq):
    n = q.shape[0]
    nq = seq // tq
    return pl.pallas_call(
        _attn_kernel,
        grid=(batch, N_HEADS, nq),
        in_specs=[pl.BlockSpec((tq, LANE), lambda b, h, i: (b * nq + i, h)),
                  pl.BlockSpec((seq, LANE), lambda b, h, i: (b, _kv_index(h))),
                  pl.BlockSpec((seq, LANE), lambda b, h, i: (b, _kv_index(h)))],
        out_specs=pl.BlockSpec((tq, LANE), lambda b, h, i: (b * nq + i, h)),
        out_shape=jax.ShapeDtypeStruct((n, N_HEADS * LANE), BF16),
        compiler_params=_params("parallel", "parallel", "parallel"),
        name="attention",
    )(q, k, v)


def _attn_out_kernel(o_ref, x_ref, w_ref, g_ref, wr_ref, wrt_ref, x1_ref, xn_ref, aff_ref, affn_ref):
    x1 = x_ref[...] + _dot(o_ref[...], w_ref[...])
    x1_ref[...] = x1
    _router_tail(x1, g_ref, wr_ref, wrt_ref, xn_ref, aff_ref, affn_ref)


def _tail_specs(n, tile):
    out_specs = [pl.BlockSpec((tile, D_MODEL), lambda i: (i, 0)),
                 pl.BlockSpec((tile, D_MODEL), lambda i: (i, 0)),
                 pl.BlockSpec((N_EXPERTS, tile), lambda i: (0, i)),
                 pl.BlockSpec((tile, N_EXPERTS), lambda i: (i, 0))]
    out_shape = [jax.ShapeDtypeStruct((n, D_MODEL), F32),
                 jax.ShapeDtypeStruct((n, D_MODEL), F32),
                 jax.ShapeDtypeStruct((N_EXPERTS, n), F32),
                 jax.ShapeDtypeStruct((n, N_EXPERTS), F32)]
    return out_specs, out_shape


def _attn_out(o, x, lw, fw, tile):
    n = x.shape[0]
    full = lambda a: pl.BlockSpec(a.shape, lambda i: (0,) * a.ndim)
    out_specs, out_shape = _tail_specs(n, tile)
    weights = (lw["w_out"], fw["g"], fw["wr"], fw["wrt"])
    return pl.pallas_call(
        _attn_out_kernel,
        grid=(n // tile,),
        in_specs=[pl.BlockSpec((tile, N_HEADS * LANE), lambda i: (i, 0)),
                  pl.BlockSpec((tile, D_MODEL), lambda i: (i, 0))] + [full(a) for a in weights],
        out_specs=out_specs,
        out_shape=out_shape,
        compiler_params=_params("parallel"),
        name="attn_out",
    )(o, x, *weights)


def _sgu_kernel(x_ref, g_ref, win_ref, lng_ref, lnb_ref, ws_ref, bs_ref, wout_ref, gf_ref, wr_ref, wrt_ref,
                x1_ref, xn_ref, aff_ref, affn_ref):
    x = x_ref[...]
    tile = x.shape[0]
    z = jax.nn.gelu(_dot(_rms(x, g_ref[...]), win_ref[...]))
    u, v = z[:, :C_WIDTH], z[:, C_WIDTH:]
    mu = jnp.mean(v, axis=-1, keepdims=True)
    vc = v - mu
    v = vc * lax.rsqrt(jnp.mean(vc * vc, axis=-1, keepdims=True) + EPS) * lng_ref[...] + lnb_ref[...]
    v = v.astype(BF16)
    bias = bs_ref[...]
    rows = []
    for c in range(tile // C_CHUNK):
        vch = v[c * C_CHUNK:(c + 1) * C_CHUNK, :]
        cols = [_dot(ws_ref[g], vch[:, g * C_GROUP_W:(g + 1) * C_GROUP_W]) for g in range(C_GROUPS)]
        rows.append(jnp.concatenate(cols, axis=1) + bias)
    s = jnp.concatenate(rows, axis=0)
    x1 = x + _dot(u * s, wout_ref[...])
    x1_ref[...] = x1
    _router_tail(x1, gf_ref, wr_ref, wrt_ref, xn_ref, aff_ref, affn_ref)


def _sgu(x, lw, fw, tile):
    n = x.shape[0]
    full = lambda a: pl.BlockSpec(a.shape, lambda i: (0,) * a.ndim)
    out_specs, out_shape = _tail_specs(n, tile)
    weights = (lw["g"], lw["w_in"], lw["ln_g"], lw["ln_b"], lw["w_s"], lw["b_s"], lw["w_out"],
               fw["g"], fw["wr"], fw["wrt"])
    return pl.pallas_call(
        _sgu_kernel,
        grid=(n // tile,),
        in_specs=[pl.BlockSpec((tile, D_MODEL), lambda i: (i, 0))] + [full(a) for a in weights],
        out_specs=out_specs,
        out_shape=out_shape,
        compiler_params=_params("parallel"),
        name="sgu",
    )(x, *weights)


def _threshold_kernel(aff_ref, thr_ref, need_ref, *, cap):
    def count(pred):
        return jnp.sum(jnp.where(pred, 1.0, 0.0), axis=1, keepdims=True)

    def body(i, cur):
        cand = cur | jnp.left_shift(jnp.int32(1), 30 - i)
        bits = lax.bitcast_convert_type(aff_ref[...], jnp.int32)
        return jnp.where(count(bits >= cand) >= cap, cand, cur)

    thr = lax.fori_loop(0, 31, body, jnp.zeros((N_EXPERTS, 1), jnp.int32))
    bits = lax.bitcast_convert_type(aff_ref[...], jnp.int32)
    thr_ref[...] = thr
    need_ref[...] = cap - count(bits > thr)


def _threshold(aff_t, cap):
    n = aff_t.shape[1]
    small = lambda dt: (pl.BlockSpec((N_EXPERTS, 1), lambda i: (0, 0)),
                        jax.ShapeDtypeStruct((N_EXPERTS, 1), dt))
    (s0, o0), (s1, o1) = small(jnp.int32), small(F32)
    return pl.pallas_call(
        functools.partial(_threshold_kernel, cap=cap),
        grid=(1,),
        in_specs=[pl.BlockSpec((N_EXPERTS, n), lambda i: (0, 0))],
        out_specs=[s0, s1],
        out_shape=[o0, o1],
        compiler_params=_params("arbitrary"),
        name="ec_threshold",
    )(aff_t)


def _compact_kernel(aff_ref, affn_ref, thr_ref, need_ref, idx_ref, gate_ref, offs_ref, carry_ref, tie_ref,
                    *, chunks, step_tokens):
    step = pl.program_id(0)

    @pl.when(step == 0)
    def _():
        idx_ref[...] = jnp.zeros_like(idx_ref)
        gate_ref[...] = jnp.zeros_like(gate_ref)
        carry_ref[...] = jnp.zeros_like(carry_ref)
        tie_ref[...] = jnp.zeros_like(tie_ref)

    offs_ref[0] = jnp.broadcast_to(carry_ref[...], (8, LANE))

    row_i = lax.broadcasted_iota(jnp.int32, (SEL_BLOCK, SEL_BLOCK), 0)
    col_i = lax.broadcasted_iota(jnp.int32, (SEL_BLOCK, SEL_BLOCK), 1)
    upper = jnp.where(row_i < col_i, 1.0, 0.0).astype(BF16)
    lower = jnp.where(col_i < row_i, 1.0, 0.0).astype(BF16)
    eye = jnp.where(row_i == col_i, 1.0, 0.0).astype(BF16)
    lane_f = col_i.astype(F32)
    sub_f = row_i[:, :1].astype(F32)
    nt = (((1,), (1,)), ((), ()))
    thr = thr_ref[...]
    need = need_ref[...]

    for sb in range(step_tokens // SEL_BLOCK):
        bits = lax.bitcast_convert_type(aff_ref[:, sb * SEL_BLOCK:(sb + 1) * SEL_BLOCK], jnp.int32)
        eq = bits == thr
        eq_f = jnp.where(eq, 1.0, 0.0)
        ties_before = jnp.dot(eq_f.astype(BF16), upper, preferred_element_type=F32) + tie_ref[...]
        sel = jnp.where(bits > thr, 1.0, jnp.where(eq & (ties_before < need), 1.0, 0.0))
        tie_ref[...] = tie_ref[...] + jnp.sum(eq_f, axis=1, keepdims=True)

        sel_pad = jnp.concatenate([sel, jnp.zeros((SEL_BLOCK - N_EXPERTS, SEL_BLOCK), F32)],
                                  axis=0).astype(BF16)
        sel_t = lax.dot_general(eye, sel_pad, nt, preferred_element_type=F32)
        carry = carry_ref[...]
        slot_t = lax.dot_general(lower, sel_pad, nt, preferred_element_type=F32) + carry
        tok = (step * step_tokens + sb * SEL_BLOCK).astype(F32) + sub_f
        affn = affn_ref[sb * SEL_BLOCK:(sb + 1) * SEL_BLOCK, :]

        for e in range(N_EXPERTS):
            chunk = carry[0, e].astype(jnp.int32) // SEL_BLOCK
            rel = jnp.where(sel_t[:, e:e + 1] > 0.5,
                            slot_t[:, e:e + 1] - (chunk * SEL_BLOCK).astype(F32), -1.0)
            lo = rel == lane_f
            hi = rel == lane_f + SEL_BLOCK
            g = affn[:, e:e + 1]
            row = e * chunks + chunk
            spread = lambda m, v: jnp.sum(jnp.where(m, v, 0.0), axis=0, keepdims=True)
            idx_ref[row] = idx_ref[row] + spread(lo, tok)
            idx_ref[row + 1] = idx_ref[row + 1] + spread(hi, tok)
            gate_ref[row] = gate_ref[row] + spread(lo, g)
            gate_ref[row + 1] = gate_ref[row + 1] + spread(hi, g)

        carry_ref[...] = carry + jnp.sum(sel_t, axis=0, keepdims=True)


def _compact(aff_t, aff_n, thr, need, chunks, step_tokens):
    n = aff_t.shape[1]
    n_steps = n // step_tokens
    n_rows = N_EXPERTS * chunks + 1
    small = lambda a: pl.BlockSpec(a.shape, lambda i: (0, 0))
    slots = pl.BlockSpec((n_rows, 1, SEL_BLOCK), lambda i: (0, 0, 0))
    return pl.pallas_call(
        functools.partial(_compact_kernel, chunks=chunks, step_tokens=step_tokens),
        grid=(n_steps,),
        in_specs=[pl.BlockSpec((N_EXPERTS, step_tokens), lambda i: (0, i)),
                  pl.BlockSpec((step_tokens, N_EXPERTS), lambda i: (i, 0)),
                  small(thr), small(need)],
        out_specs=[slots, slots, pl.BlockSpec((1, 8, LANE), lambda i: (i, 0, 0))],
        out_shape=[jax.ShapeDtypeStruct((n_rows, 1, SEL_BLOCK), F32),
                   jax.ShapeDtypeStruct((n_rows, 1, SEL_BLOCK), F32),
                   jax.ShapeDtypeStruct((n_steps, 8, LANE), F32)],
        scratch_shapes=[pltpu.VMEM((1, LANE), F32), pltpu.VMEM((N_EXPERTS, 1), F32)],
        compiler_params=_params("arbitrary"),
        name="ec_compact",
    )(aff_t, aff_n, thr, need)


def _ffn_kernel(idx_ref, gate_ref, xn_hbm, wg_ref, wu_ref, wd_ref, y_ref, buf0, buf1, sem, *, rows, n_pairs):
    pair = pl.program_id(0) * pl.num_programs(1) + pl.program_id(1)

    def row_copy(tok, buf, r, s):
        return pltpu.make_async_copy(xn_hbm.at[pl.ds(tok, 1), :], buf.at[pl.ds(r, 1), :], sem.at[s])

    def gather(chunk, buf, s):
        for r in range(rows):
            row_copy(idx_ref[chunk * rows + r], buf, r, s).start()

    def wait(buf, s):
        pltpu.make_async_copy(xn_hbm.at[pl.ds(0, rows), :], buf, sem.at[s]).wait()

    def compute(buf, half):
        xg = buf[...].astype(BF16)
        hid = jax.nn.silu(_dot(xg, wg_ref[0])) * _dot(xg, wu_ref[0])
        y = _dot(hid, wd_ref[0]) * gate_ref[half * rows:(half + 1) * rows, :]
        y_ref[half * rows:(half + 1) * rows] = y.reshape(rows, 1, D_MODEL)

    @pl.when(pair == 0)
    def _():
        def body(r, c):
            row_copy(idx_ref[r], buf0, r, 0).start()
            return c
        lax.fori_loop(0, rows, body, 0)

    wait(buf0, 0)
    gather(2 * pair + 1, buf1, 1)
    compute(buf0, 0)
    wait(buf1, 1)
    gather(jnp.where(pair + 1 < n_pairs, 2 * pair + 2, 0), buf0, 0)
    compute(buf1, 1)

    @pl.when(pair == n_pairs - 1)
    def _():
        wait(buf0, 0)


def _ffn(xn, idx, gate, fw, cap, rows):
    pairs = cap // (2 * rows)
    w_spec = lambda shape: pl.BlockSpec((1,) + shape, lambda e, c, idx: (e, 0, 0))
    grid_spec = pltpu.PrefetchScalarGridSpec(
        num_scalar_prefetch=1,
        grid=(N_EXPERTS, pairs),
        in_specs=[pl.BlockSpec((2 * rows, 1), lambda e, c, idx: (e * pairs + c, 0)),
                  pl.BlockSpec(memory_space=pl.ANY),
                  w_spec((D_MODEL, EXPERT_FF)), w_spec((D_MODEL, EXPERT_FF)), w_spec((EXPERT_FF, D_MODEL))],
        out_specs=pl.BlockSpec((2 * rows, 1, D_MODEL), lambda e, c, idx: (e * pairs + c, 0, 0)),
        scratch_shapes=[pltpu.VMEM((rows, D_MODEL), F32), pltpu.VMEM((rows, D_MODEL), F32),
                        pltpu.SemaphoreType.DMA((2,))],
    )
    return pl.pallas_call(
        functools.partial(_ffn_kernel, rows=rows, n_pairs=N_EXPERTS * pairs),
        grid_spec=grid_spec,
        out_shape=jax.ShapeDtypeStruct((N_EXPERTS * cap, 1, D_MODEL), F32),
        compiler_params=pltpu.CompilerParams(dimension_semantics=("arbitrary", "arbitrary"),
                                             vmem_limit_bytes=VMEM_LIMIT, disable_bounds_checks=True),
        name="ec_ffn",
    )(idx, gate, xn, fw["w_gate"], fw["w_up"], fw["w_down"])


def _combine_kernel(offs_ref, idx_ref, x1_ref, y_hbm, o_ref, acc, ywin, yover, sem, osem,
                    *, cap, n_tiles, tile, width):
    j = pl.program_id(0)
    slot = j % 2
    total = N_EXPERTS * cap

    def window(t, e):
        first = e * cap + offs_ref[t * N_EXPERTS + e]
        return first, jnp.minimum(first, total - width)

    def win_copy(t, e, s):
        return pltpu.make_async_copy(y_hbm.at[pl.ds(window(t, e)[1], width)], ywin.at[s, e], sem.at[s])

    def fetch(t, s):
        for e in range(N_EXPERTS):
            win_copy(t, e, s).start()

    def add_rows(win, first, shift, count):
        def body(g, c):
            work = []
            for u in range(ROW_UNROLL):
                r = g * ROW_UNROLL + u
                tok = idx_ref[jnp.minimum(first + r, total - 1)] - j * tile
                dst = jnp.where(r < count, tok, tile + u)
                work.append((dst, acc[dst] + win[jnp.minimum(shift + r, width - 1)]))
            for dst, val in work:
                acc[dst] = val
            return c
        lax.fori_loop(0, (count + ROW_UNROLL - 1) // ROW_UNROLL, body, 0)

    @pl.when(j == 0)
    def _():
        fetch(0, 0)

    @pl.when(j + 1 < n_tiles)
    def _():
        fetch(j + 1, 1 - slot)

    acc[...] = jnp.zeros_like(acc)
    for e in range(N_EXPERTS):
        win_copy(j, e, slot).wait()

    for e in range(N_EXPERTS):
        first, start = window(j, e)
        count = offs_ref[(j + 1) * N_EXPERTS + e] - offs_ref[j * N_EXPERTS + e]
        add_rows(ywin.at[slot, e], first, first - start, jnp.minimum(count, width))

        def overflow(k, c, first=first, count=count):
            begin = first + k * width
            start = jnp.minimum(begin, total - width)
            cp = pltpu.make_async_copy(y_hbm.at[pl.ds(start, width)], yover, osem)
            cp.start()
            cp.wait()
            add_rows(yover, begin, begin - start, jnp.minimum(count - k * width, width))
            return c
        lax.fori_loop(1, (count + width - 1) // width, overflow, 0)

    o_ref[...] = x1_ref[...] + acc[0:tile].reshape(tile, D_MODEL)


def _combine(x1, y, idx, offs, cap, tile, width):
    n = x1.shape[0]
    n_tiles = n // tile
    grid_spec = pltpu.PrefetchScalarGridSpec(
        num_scalar_prefetch=2,
        grid=(n_tiles,),
        in_specs=[pl.BlockSpec((tile, D_MODEL), lambda i, offs, idx: (i, 0)),
                  pl.BlockSpec(memory_space=pl.ANY)],
        out_specs=pl.BlockSpec((tile, D_MODEL), lambda i, offs, idx: (i, 0)),
        scratch_shapes=[pltpu.VMEM((tile + ROW_UNROLL, 1, D_MODEL), F32),
                        pltpu.VMEM((2, N_EXPERTS, width, 1, D_MODEL), F32),
                        pltpu.VMEM((width, 1, D_MODEL), F32),
                        pltpu.SemaphoreType.DMA((2,)), pltpu.SemaphoreType.DMA(())],
    )
    return pl.pallas_call(
        functools.partial(_combine_kernel, cap=cap, n_tiles=n_tiles, tile=tile, width=width),
        grid_spec=grid_spec,
        out_shape=jax.ShapeDtypeStruct((n, D_MODEL), F32),
        compiler_params=_params("arbitrary"),
        name="ec_combine",
    )(offs, idx, x1, y)


def _ec_ffn(x1, xn, aff_t, aff_n, fw):
    n = x1.shape[0]
    cap = EC_CAPACITY_FACTOR * n // N_EXPERTS
    chunks = cap // SEL_BLOCK
    step_tokens = min(SEL_STEP, n)
    thr, need = _threshold(aff_t, cap)
    idx_f, gate_f, offs_f = _compact(aff_t, aff_n, thr, need, chunks, step_tokens)
    idx = idx_f[:N_EXPERTS * chunks].reshape(-1).astype(jnp.int32)
    gate = gate_f[:N_EXPERTS * chunks].reshape(-1, 1)
    offs = jnp.concatenate([offs_f[:, 0, :N_EXPERTS].astype(jnp.int32),
                            jnp.full((1, N_EXPERTS), cap, jnp.int32)], axis=0).reshape(-1)
    y = _ffn(xn, idx, gate, fw, cap, min(FFN_ROWS, cap // 2))
    return _combine(x1, y, idx, offs, cap, step_tokens, min(COMBINE_W, cap))


def _final_norm_kernel(x_ref, g_ref, o_ref):
    o_ref[...] = _rms(x_ref[...], g_ref[...])


def _final_norm(x, g, tile):
    n = x.shape[0]
    return pl.pallas_call(
        _final_norm_kernel,
        grid=(n // tile,),
        in_specs=[pl.BlockSpec((tile, D_MODEL), lambda i: (i, 0)),
                  pl.BlockSpec((1, D_MODEL), lambda i: (0, 0))],
        out_specs=pl.BlockSpec((tile, D_MODEL), lambda i: (i, 0)),
        out_shape=jax.ShapeDtypeStruct((n, D_MODEL), F32),
        compiler_params=_params("parallel"),
        name="final_norm",
    )(x, g)


def _pad_heads(w, n_heads, width):
    lead = w.shape[:-1]
    w = w.reshape(lead + (n_heads, width))
    w = jnp.pad(w, [(0, 0)] * len(lead) + [(0, 0), (0, LANE - width)])
    return w.reshape(lead + (n_heads * LANE,))


def _rope_tables(seq):
    rows = seq // GRID_W
    row = jnp.repeat(jnp.arange(rows, dtype=F32), GRID_W)
    col = jnp.tile(jnp.arange(GRID_W, dtype=F32), rows)

    def cos_sin(rot_dim):
        n_pair = rot_dim // 4
        freq = ROPE_THETA ** (-jnp.arange(n_pair, dtype=F32) / n_pair)
        ang = jnp.concatenate([row[:, None] * freq, col[:, None] * freq], axis=-1)
        return jnp.cos(ang), jnp.sin(ang)

    def place(parts):
        out = jnp.zeros((seq, LANE), F32)
        for start, val in parts:
            out = out.at[:, start:start + val.shape[1]].set(val)
        return out

    ca, sa = cos_sin(A_HEAD_DIM)
    cb, sb = cos_sin(B_ROPE)
    ha, hb = A_HEAD_DIM // 2, B_ROPE // 2
    ones = jnp.ones((seq, B_NOPE), F32)
    return (place([(0, ca), (ha, ca)]), place([(0, -sa)]), place([(ha, sa)]),
            place([(0, ones), (B_NOPE, cb), (B_NOPE + hb, cb)]), place([(B_NOPE, -sb)]),
            place([(B_NOPE + hb, sb)]))


def _prep_attn_layer(i, p, rope):
    w_in = p["w_in_attn"][i]
    o = 0
    w_qa = w_in[:, o:o + A_Q_W]; o += A_Q_W
    w_ka = w_in[:, o:o + A_KV_W]; o += A_KV_W
    w_va = w_in[:, o:o + A_KV_W]; o += A_KV_W
    w_cq = w_in[:, o:o + B_Q_RANK]; o += B_Q_RANK
    w_ckv = w_in[:, o:o + B_KV_RANK]; o += B_KV_RANK
    w_kr = w_in[:, o:o + B_ROPE]
    w_kr = jnp.pad(w_kr, [(0, 0), (B_NOPE, LANE - B_NOPE - B_ROPE)])
    w_all = jnp.concatenate([_pad_heads(w_qa, A_HEADS, A_HEAD_DIM), _pad_heads(w_ka, A_KV_HEADS, A_HEAD_DIM),
                             _pad_heads(w_va, A_KV_HEADS, A_HEAD_DIM), w_cq, w_ckv, w_kr], axis=1)
    w_ukv = p["w_ukv"][i].reshape(B_KV_RANK, B_HEADS, B_NOPE + B_V)
    w_uk = _pad_heads(w_ukv[:, :, :B_NOPE].reshape(B_KV_RANK, -1), B_HEADS, B_NOPE)
    w_uv = _pad_heads(w_ukv[:, :, B_NOPE:].reshape(B_KV_RANK, -1), B_HEADS, B_V)
    w_out = p["w_out_attn"][i].reshape(N_HEADS, A_HEAD_DIM, D_MODEL)
    w_out = jnp.pad(w_out, [(0, 0), (0, LANE - A_HEAD_DIM), (0, 0)]).reshape(N_HEADS * LANE, D_MODEL)
    return {
        "g": p["attn_norm"][i][None, :],
        "w_all": w_all.astype(BF16),
        "gq": jnp.tile(jnp.pad(p["qk_norm_q"][i], (0, LANE - A_HEAD_DIM)), A_HEADS)[None, :],
        "gk": jnp.tile(jnp.pad(p["qk_norm_k"][i], (0, LANE - A_HEAD_DIM)), A_KV_HEADS)[None, :],
        "mq": p["mla_q_norm"][i][None, :],
        "wuq": _pad_heads(p["w_uq"][i], B_HEADS, B_QK).astype(BF16),
        "mkv": p["mla_kv_norm"][i][None, :],
        "wukv": jnp.concatenate([w_uk, w_uv], axis=1).astype(BF16),
        "w_out": w_out.astype(BF16),
        "rope": rope,
    }


def _prep_sgu_layer(i, p):
    b_s = p["b_spatial"][i]
    return {
        "g": p["sgu_norm"][i][None, :],
        "w_in": p["w_in_sgu"][i].astype(BF16),
        "ln_g": p["sgu_ln_g"][i][None, :],
        "ln_b": p["sgu_ln_b"][i][None, :],
        "w_s": p["w_spatial"][i].astype(BF16),
        "b_s": jnp.repeat(b_s.T, C_GROUP_W, axis=1),
        "w_out": p["w_out_sgu"][i].astype(BF16),
    }


def _prep_ffn_layer(l, p, w_gate, w_up, w_down):
    return {
        "g": p["ffn_norm"][l][None, :],
        "wr": p["w_router"][l].astype(BF16),
        "wrt": p["w_router"][l].T.astype(BF16),
        "w_gate": w_gate[l], "w_up": w_up[l], "w_down": w_down[l],
    }


def _pick(n, target):
    t = min(n, target)
    while n % t:
        t //= 2
    return t


def _trunk(x, attn_layers, sgu_layers, ffn_layers, final_g):
    batch, seq, _ = x.shape
    n = batch * seq
    x = x.reshape(n, D_MODEL)
    tile = _pick(seq, 512)
    tq = _pick(seq, 1024)
    for l, fw in enumerate(ffn_layers):
        if l % 2 == 0:
            lw = attn_layers[l // 2]
            q, k, v = _attn_proj(x, lw, seq, tile)
            o = _attention(q, k, v, batch, seq, tq)
            x1, xn, aff_t, aff_n = _attn_out(o, x, lw, fw, tile)
        else:
            x1, xn, aff_t, aff_n = _sgu(x, sgu_layers[l // 2], fw, tile)
        x = _ec_ffn(x1, xn, aff_t, aff_n, fw)
    return _final_norm(x, final_g[None, :], tile).reshape(batch, seq, D_MODEL)


def kernel(x_prompt, x_sample, attn_norm, w_in_attn, qk_norm_q, qk_norm_k, mla_q_norm, w_uq, mla_kv_norm, w_ukv, w_out_attn, sgu_norm, w_in_sgu, sgu_ln_g, sgu_ln_b, w_spatial, b_spatial, w_out_sgu, ffn_norm, w_router, w_gate, w_up, w_down, final_norm):
    p = dict(attn_norm=attn_norm, w_in_attn=w_in_attn, qk_norm_q=qk_norm_q, qk_norm_k=qk_norm_k,
             mla_q_norm=mla_q_norm, w_uq=w_uq, mla_kv_norm=mla_kv_norm, w_ukv=w_ukv,
             w_out_attn=w_out_attn, sgu_norm=sgu_norm, w_in_sgu=w_in_sgu, sgu_ln_g=sgu_ln_g,
             sgu_ln_b=sgu_ln_b, w_spatial=w_spatial, b_spatial=b_spatial, w_out_sgu=w_out_sgu,
             ffn_norm=ffn_norm, w_router=w_router)
    depth = ffn_norm.shape[0]
    wg, wu, wd = w_gate.astype(BF16), w_up.astype(BF16), w_down.astype(BF16)
    ffn_layers = [_prep_ffn_layer(l, p, wg, wu, wd) for l in range(depth)]
    sgu_layers = [_prep_sgu_layer(i, p) for i in range(sgu_norm.shape[0])]
    outs = []
    for x in (x_prompt, x_sample):
        rope = _rope_tables(x.shape[1])
        attn_layers = [_prep_attn_layer(i, p, rope) for i in range(attn_norm.shape[0])]
        outs.append(_trunk(x, attn_layers, sgu_layers, ffn_layers, final_norm))
    return tuple(outs)
```

```python
import functools

import jax
import jax.numpy as jnp
from jax import lax
from jax.experimental import pallas as pl
from jax.experimental.pallas import tpu as pltpu

D_MODEL = 1024
GRID_W = 64
ROPE_THETA = 10000.0
EPS = 1e-6

A_HEADS = 8
A_KV_HEADS = 2
A_GROUP = A_HEADS // A_KV_HEADS
A_HEAD_DIM = 64
A_Q_W = A_HEADS * A_HEAD_DIM
A_KV_W = A_KV_HEADS * A_HEAD_DIM

B_HEADS = 8
B_Q_RANK = 384
B_KV_RANK = 256
B_NOPE = 64
B_ROPE = 32
B_V = 64
B_QK = B_NOPE + B_ROPE

N_HEADS = A_HEADS + B_HEADS
HEADS_PER_STEP = 2
LOG2_E = 1.4426950408889634

C_WIDTH = D_MODEL
C_GROUPS = 8
C_GROUP_W = C_WIDTH // C_GROUPS
C_CHUNK = 128

N_EXPERTS = 16
EXPERT_FF = 2 * D_MODEL
EC_CAPACITY_FACTOR = 2

LANE = 128
VMEM_LIMIT = 56 * 1024 * 1024
SEL_BLOCK = LANE
SEL_STEP = 512
FFN_ROWS = 512
COMBINE_W = 96
ROW_UNROLL = 8

BF16 = jnp.bfloat16
F32 = jnp.float32


def _params(*sem):
    return pltpu.CompilerParams(dimension_semantics=sem, vmem_limit_bytes=VMEM_LIMIT)


def _dot(a, b):
    return jnp.dot(a.astype(BF16), b.astype(BF16), preferred_element_type=F32)


def _dot_nt(a, b):
    return lax.dot_general(a.astype(BF16), b.astype(BF16), (((1,), (1,)), ((), ())),
                           preferred_element_type=F32)


def _rms(x, g):
    return x * lax.rsqrt(jnp.mean(x * x, axis=-1, keepdims=True) + EPS) * g


def _router_tail(x1, g_ref, wrt_ref, xn_ref, aff_ref):
    xn = _rms(x1, g_ref[...])
    xn_ref[...] = xn
    logits = _dot_nt(wrt_ref[...], xn)
    m = jnp.max(logits, axis=0, keepdims=True)
    e = jnp.exp(logits - m)
    aff_ref[...] = e / jnp.sum(e, axis=0, keepdims=True)


def _head_norm(z, n_heads, width):
    outs = []
    for h in range(n_heads):
        blk = z[:, h * LANE:(h + 1) * LANE]
        ms = jnp.sum(blk * blk, axis=-1, keepdims=True) * (1.0 / width)
        outs.append(blk * lax.rsqrt(ms + EPS))
    return jnp.concatenate(outs, axis=1)


def _rope(x, c, s_lo, s_hi, half, n_heads):
    w = x.shape[1]
    c = jnp.tile(c, (1, n_heads))
    s_lo = jnp.tile(s_lo, (1, n_heads))
    s_hi = jnp.tile(s_hi, (1, n_heads))
    return x * c + pltpu.roll(x, w - half, 1) * s_lo + pltpu.roll(x, half, 1) * s_hi


def _attn_proj_kernel(x_ref, g_ref, w_ref, gq_ref, gk_ref, mq_ref, wuq_ref, mkv_ref, wukv_ref,
                      ca_ref, sa1_ref, sa2_ref, cb_ref, sb1_ref, sb2_ref,
                      q_ref, k_ref, v_ref):
    h = _rms(x_ref[...], g_ref[...])
    z = _dot(h, w_ref[...])
    o = 0
    qa = z[:, o:o + A_HEADS * LANE]; o += A_HEADS * LANE
    ka = z[:, o:o + A_KV_HEADS * LANE]; o += A_KV_HEADS * LANE
    va = z[:, o:o + A_KV_HEADS * LANE]; o += A_KV_HEADS * LANE
    cq = z[:, o:o + B_Q_RANK]; o += B_Q_RANK
    ckv = z[:, o:o + B_KV_RANK]; o += B_KV_RANK
    kr = z[:, o:o + LANE]

    ca, sa1, sa2 = ca_ref[...], sa1_ref[...], sa2_ref[...]
    cb, sb1, sb2 = cb_ref[...], sb1_ref[...], sb2_ref[...]

    qa = _rope(_head_norm(qa, A_HEADS, A_HEAD_DIM) * gq_ref[...], ca, sa1, sa2, A_HEAD_DIM // 2, A_HEADS)
    ka = _rope(_head_norm(ka, A_KV_HEADS, A_HEAD_DIM) * gk_ref[...], ca, sa1, sa2, A_HEAD_DIM // 2,
               A_KV_HEADS)

    qb = _rope(_dot(_rms(cq, mq_ref[...]), wuq_ref[...]), cb, sb1, sb2, B_ROPE // 2, B_HEADS)
    kvb = _dot(_rms(ckv, mkv_ref[...]), wukv_ref[...])
    kr = _rope(kr, cb, sb1, sb2, B_ROPE // 2, 1)
    kb = kvb[:, :B_HEADS * LANE] + jnp.tile(kr, (1, B_HEADS))
    vb = kvb[:, B_HEADS * LANE:]

    q_ref[...] = jnp.concatenate([qa, qb], axis=1).astype(BF16)
    per_query_head = lambda x: [x[:, g * LANE:(g + 1) * LANE] for g in range(A_KV_HEADS)
                                for _ in range(A_GROUP)]
    k_ref[...] = jnp.concatenate(per_query_head(ka) + [kb], axis=1).astype(BF16)
    v_ref[...] = jnp.concatenate(per_query_head(va) + [vb], axis=1).astype(BF16)


def _attn_proj(x, lw, seq, tile):
    n = x.shape[0]
    nt = n // tile
    per_seq = seq // tile
    full = lambda a: pl.BlockSpec(a.shape, lambda i: (0,) * a.ndim)
    tok = lambda w: pl.BlockSpec((tile, w), lambda i: (i, 0))
    tab = pl.BlockSpec((tile, LANE), lambda i: (i % per_seq, 0))
    weights = (lw["g"], lw["w_all"], lw["gq"], lw["gk"], lw["mq"], lw["wuq"], lw["mkv"], lw["wukv"])
    return pl.pallas_call(
        _attn_proj_kernel,
        grid=(nt,),
        in_specs=[tok(D_MODEL)] + [full(a) for a in weights] + [tab] * 6,
        out_specs=[tok(N_HEADS * LANE)] * 3,
        out_shape=[jax.ShapeDtypeStruct((n, N_HEADS * LANE), BF16)] * 3,
        compiler_params=_params("parallel"),
        name="attn_proj",
    )(x, *weights, *lw["rope"])


def _attn_kernel(q_ref, k_ref, v_ref, o_ref):
    pair = pl.program_id(1)
    scale = jnp.where(pair < A_HEADS // HEADS_PER_STEP, A_HEAD_DIM ** -0.5, B_QK ** -0.5).astype(F32)
    c = scale * LOG2_E
    heads = [slice(h * LANE, (h + 1) * LANE) for h in range(HEADS_PER_STEP)]
    scores = [_dot_nt(q_ref[:, sl], k_ref[:, sl]) for sl in heads]
    outs = []
    for s, sl in zip(scores, heads):
        m = jnp.max(s, axis=-1, keepdims=True)
        p = jnp.exp2((s - m) * c)
        l = jnp.sum(p, axis=-1, keepdims=True)
        outs.append(_dot(p, v_ref[:, sl]) / l)
    o_ref[...] = jnp.concatenate(outs, axis=1).astype(BF16)


def _attention(q, k, v, batch, seq, tq):
    n = q.shape[0]
    nq = seq // tq
    width = HEADS_PER_STEP * LANE
    return pl.pallas_call(
        _attn_kernel,
        grid=(batch, N_HEADS // HEADS_PER_STEP, nq),
        in_specs=[pl.BlockSpec((tq, width), lambda b, h, i: (b * nq + i, h)),
                  pl.BlockSpec((seq, width), lambda b, h, i: (b, h)),
                  pl.BlockSpec((seq, width), lambda b, h, i: (b, h))],
        out_specs=pl.BlockSpec((tq, width), lambda b, h, i: (b * nq + i, h)),
        out_shape=jax.ShapeDtypeStruct((n, N_HEADS * LANE), BF16),
        compiler_params=_params("parallel", "parallel", "parallel"),
        name="attention",
    )(q, k, v)


def _attn_out_kernel(o_ref, x_ref, w_ref, g_ref, wrt_ref, x1_ref, xn_ref, aff_ref):
    x1 = x_ref[...] + _dot(o_ref[...], w_ref[...])
    x1_ref[...] = x1
    _router_tail(x1, g_ref, wrt_ref, xn_ref, aff_ref)


def _tail_specs(n, tile):
    out_specs = [pl.BlockSpec((tile, D_MODEL), lambda i: (i, 0)),
                 pl.BlockSpec((tile, D_MODEL), lambda i: (i, 0)),
                 pl.BlockSpec((N_EXPERTS, tile), lambda i: (0, i))]
    out_shape = [jax.ShapeDtypeStruct((n, D_MODEL), F32),
                 jax.ShapeDtypeStruct((n, D_MODEL), F32),
                 jax.ShapeDtypeStruct((N_EXPERTS, n), F32)]
    return out_specs, out_shape


def _attn_out(o, x, lw, fw, tile):
    n = x.shape[0]
    full = lambda a: pl.BlockSpec(a.shape, lambda i: (0,) * a.ndim)
    out_specs, out_shape = _tail_specs(n, tile)
    weights = (lw["w_out"], fw["g"], fw["wrt"])
    return pl.pallas_call(
        _attn_out_kernel,
        grid=(n // tile,),
        in_specs=[pl.BlockSpec((tile, N_HEADS * LANE), lambda i: (i, 0)),
                  pl.BlockSpec((tile, D_MODEL), lambda i: (i, 0))] + [full(a) for a in weights],
        out_specs=out_specs,
        out_shape=out_shape,
        compiler_params=_params("parallel"),
        name="attn_out",
    )(o, x, *weights)


def _sgu_kernel(x_ref, g_ref, win_ref, lng_ref, lnb_ref, ws_ref, bs_ref, wout_ref, gf_ref, wrt_ref,
                x1_ref, xn_ref, aff_ref):
    x = x_ref[...]
    tile = x.shape[0]
    z = jax.nn.gelu(_dot(_rms(x, g_ref[...]), win_ref[...]))
    u, v = z[:, :C_WIDTH], z[:, C_WIDTH:]
    mu = jnp.mean(v, axis=-1, keepdims=True)
    vc = v - mu
    v = vc * lax.rsqrt(jnp.mean(vc * vc, axis=-1, keepdims=True) + EPS) * lng_ref[...] + lnb_ref[...]
    v = v.astype(BF16)
    bias = bs_ref[...]
    rows = []
    for c in range(tile // C_CHUNK):
        vch = v[c * C_CHUNK:(c + 1) * C_CHUNK, :]
        cols = [_dot(ws_ref[g], vch[:, g * C_GROUP_W:(g + 1) * C_GROUP_W]) for g in range(C_GROUPS)]
        rows.append(jnp.concatenate(cols, axis=1) + bias)
    s = jnp.concatenate(rows, axis=0)
    x1 = x + _dot(u * s, wout_ref[...])
    x1_ref[...] = x1
    _router_tail(x1, gf_ref, wrt_ref, xn_ref, aff_ref)


def _sgu(x, lw, fw, tile):
    n = x.shape[0]
    full = lambda a: pl.BlockSpec(a.shape, lambda i: (0,) * a.ndim)
    out_specs, out_shape = _tail_specs(n, tile)
    weights = (lw["g"], lw["w_in"], lw["ln_g"], lw["ln_b"], lw["w_s"], lw["b_s"], lw["w_out"],
               fw["g"], fw["wrt"])
    return pl.pallas_call(
        _sgu_kernel,
        grid=(n // tile,),
        in_specs=[pl.BlockSpec((tile, D_MODEL), lambda i: (i, 0))] + [full(a) for a in weights],
        out_specs=out_specs,
        out_shape=out_shape,
        compiler_params=_params("parallel"),
        name="sgu",
    )(x, *weights)


def _threshold_kernel(aff_ref, thr_ref, need_ref, *, cap):
    def count(pred):
        return jnp.sum(jnp.where(pred, 1.0, 0.0), axis=1, keepdims=True)

    def body(i, cur):
        cand = cur | jnp.left_shift(jnp.int32(1), 30 - i)
        bits = lax.bitcast_convert_type(aff_ref[...], jnp.int32)
        return jnp.where(count(bits >= cand) >= cap, cand, cur)

    thr = lax.fori_loop(0, 31, body, jnp.zeros((N_EXPERTS, 1), jnp.int32))
    bits = lax.bitcast_convert_type(aff_ref[...], jnp.int32)
    thr_ref[...] = thr
    need_ref[...] = cap - count(bits > thr)


def _threshold(aff_t, cap):
    n = aff_t.shape[1]
    small = lambda dt: (pl.BlockSpec((N_EXPERTS, 1), lambda i: (0, 0)),
                        jax.ShapeDtypeStruct((N_EXPERTS, 1), dt))
    (s0, o0), (s1, o1) = small(jnp.int32), small(F32)
    return pl.pallas_call(
        functools.partial(_threshold_kernel, cap=cap),
        grid=(1,),
        in_specs=[pl.BlockSpec((N_EXPERTS, n), lambda i: (0, 0))],
        out_specs=[s0, s1],
        out_shape=[o0, o1],
        compiler_params=_params("arbitrary"),
        name="ec_threshold",
    )(aff_t)


def _compact_kernel(aff_ref, thr_ref, need_ref, slot_ref, offs_ref, carry_ref, tie_ref, *, chunks, step_tokens):
    step = pl.program_id(0)

    @pl.when(step == 0)
    def _():
        slot_ref[...] = jnp.zeros_like(slot_ref)
        carry_ref[...] = jnp.zeros_like(carry_ref)
        tie_ref[...] = jnp.zeros_like(tie_ref)

    offs_ref[0] = jnp.broadcast_to(carry_ref[...], (N_EXPERTS, LANE))

    row_i = lax.broadcasted_iota(jnp.int32, (SEL_BLOCK, SEL_BLOCK), 0)
    col_i = lax.broadcasted_iota(jnp.int32, (SEL_BLOCK, SEL_BLOCK), 1)
    upper = jnp.where(row_i < col_i, 1.0, 0.0).astype(BF16)
    lane = lax.broadcasted_iota(jnp.int32, (N_EXPERTS, SEL_BLOCK), 1)
    shifts = [1 << b for b in range(SEL_BLOCK.bit_length() - 1)]
    thr = thr_ref[...]
    need = need_ref[...]

    for sb in range(step_tokens // SEL_BLOCK):
        aff = aff_ref[:, sb * SEL_BLOCK:(sb + 1) * SEL_BLOCK]
        bits = lax.bitcast_convert_type(aff, jnp.int32)
        eq = bits == thr
        eq_f = jnp.where(eq, 1.0, 0.0)
        ties_before = jnp.dot(eq_f.astype(BF16), upper, preferred_element_type=F32) + tie_ref[...]
        sel = (bits > thr) | (eq & (ties_before < need))
        sel_f = jnp.where(sel, 1.0, 0.0)
        tie_ref[...] = tie_ref[...] + jnp.sum(eq_f, axis=1, keepdims=True)

        before = jnp.dot(sel_f.astype(BF16), upper, preferred_element_type=F32)
        count = jnp.sum(sel_f, axis=1, keepdims=True).astype(jnp.int32)
        carry = carry_ref[...]
        tok = (step * step_tokens + sb * SEL_BLOCK + lane).astype(F32)

        dist = jnp.where(sel, lane - before.astype(jnp.int32), 0)
        ids = jnp.where(sel, tok, 0.0)
        gates = jnp.where(sel, aff, 0.0)
        for sh in shifts:
            move = (dist & sh) != 0
            step_left = lambda x, zero: (jnp.where(move, zero, x)
                                         + pltpu.roll(jnp.where(move, x, zero), SEL_BLOCK - sh, 1))
            ids, gates, dist = step_left(ids, 0.0), step_left(gates, 0.0), step_left(dist, 0)

        off = carry & (SEL_BLOCK - 1)
        for sh in shifts:
            turn = (off & sh) != 0
            ids = jnp.where(turn, pltpu.roll(ids, sh, 1), ids)
            gates = jnp.where(turn, pltpu.roll(gates, sh, 1), gates)
        end = off + count
        in_lo = (lane >= off) & (lane < end)
        in_hi = lane < end - SEL_BLOCK
        pick = lambda m, x, e: jnp.where(m, x, 0.0)[e:e + 1, :]

        for e in range(N_EXPERTS):
            row = e * chunks + carry[e, 0] // SEL_BLOCK
            lo = jnp.concatenate([pick(in_lo, ids, e), pick(in_lo, gates, e)], axis=0)
            hi = jnp.concatenate([pick(in_hi, ids, e), pick(in_hi, gates, e)], axis=0)
            slot_ref[row] = slot_ref[row] + lo
            slot_ref[row + 1] = slot_ref[row + 1] + hi

        carry_ref[...] = carry + count


def _compact(aff_t, thr, need, chunks, step_tokens):
    n = aff_t.shape[1]
    n_steps = n // step_tokens
    n_rows = N_EXPERTS * chunks + 1
    small = lambda a: pl.BlockSpec(a.shape, lambda i: (0, 0))
    return pl.pallas_call(
        functools.partial(_compact_kernel, chunks=chunks, step_tokens=step_tokens),
        grid=(n_steps,),
        in_specs=[pl.BlockSpec((N_EXPERTS, step_tokens), lambda i: (0, i)), small(thr), small(need)],
        out_specs=[pl.BlockSpec((n_rows, 2, SEL_BLOCK), lambda i: (0, 0, 0)),
                   pl.BlockSpec((1, N_EXPERTS, LANE), lambda i: (i, 0, 0))],
        out_shape=[jax.ShapeDtypeStruct((n_rows, 2, SEL_BLOCK), F32),
                   jax.ShapeDtypeStruct((n_steps, N_EXPERTS, LANE), jnp.int32)],
        scratch_shapes=[pltpu.VMEM((N_EXPERTS, 1), jnp.int32), pltpu.VMEM((N_EXPERTS, 1), F32)],
        compiler_params=_params("arbitrary"),
        name="ec_compact",
    )(aff_t, thr, need)


def _ffn_kernel(idx_ref, gate_ref, xn_hbm, wg_ref, wu_ref, wd_ref, y_ref, buf0, buf1, sem, *, rows, n_pairs):
    pair = pl.program_id(0) * pl.num_programs(1) + pl.program_id(1)

    def row_copy(tok, buf, r, s):
        return pltpu.make_async_copy(xn_hbm.at[pl.ds(tok, 1), :], buf.at[pl.ds(r, 1), :], sem.at[s])

    def gather(chunk, buf, s):
        for r in range(rows):
            row_copy(idx_ref[chunk * rows + r], buf, r, s).start()

    def wait(buf, s):
        pltpu.make_async_copy(xn_hbm.at[pl.ds(0, rows), :], buf, sem.at[s]).wait()

    def compute(buf, half):
        xg = buf[...].astype(BF16)
        hid = jax.nn.silu(_dot(xg, wg_ref[0])) * _dot(xg, wu_ref[0])
        y = _dot(hid, wd_ref[0]) * gate_ref[half * rows:(half + 1) * rows, :]
        y_ref[half * rows:(half + 1) * rows] = y.reshape(rows, 1, D_MODEL)

    @pl.when(pair == 0)
    def _():
        def body(r, c):
            row_copy(idx_ref[r], buf0, r, 0).start()
            return c
        lax.fori_loop(0, rows, body, 0)

    wait(buf0, 0)
    gather(2 * pair + 1, buf1, 1)
    compute(buf0, 0)
    wait(buf1, 1)
    gather(jnp.where(pair + 1 < n_pairs, 2 * pair + 2, 0), buf0, 0)
    compute(buf1, 1)

    @pl.when(pair == n_pairs - 1)
    def _():
        wait(buf0, 0)


def _ffn(xn, idx, gate, fw, cap, rows):
    pairs = cap // (2 * rows)
    w_spec = lambda shape: pl.BlockSpec((1,) + shape, lambda e, c, idx: (e, 0, 0))
    grid_spec = pltpu.PrefetchScalarGridSpec(
        num_scalar_prefetch=1,
        grid=(N_EXPERTS, pairs),
        in_specs=[pl.BlockSpec((2 * rows, 1), lambda e, c, idx: (e * pairs + c, 0)),
                  pl.BlockSpec(memory_space=pl.ANY),
                  w_spec((D_MODEL, EXPERT_FF)), w_spec((D_MODEL, EXPERT_FF)), w_spec((EXPERT_FF, D_MODEL))],
        out_specs=pl.BlockSpec((2 * rows, 1, D_MODEL), lambda e, c, idx: (e * pairs + c, 0, 0)),
        scratch_shapes=[pltpu.VMEM((rows, D_MODEL), F32), pltpu.VMEM((rows, D_MODEL), F32),
                        pltpu.SemaphoreType.DMA((2,))],
    )
    return pl.pallas_call(
        functools.partial(_ffn_kernel, rows=rows, n_pairs=N_EXPERTS * pairs),
        grid_spec=grid_spec,
        out_shape=jax.ShapeDtypeStruct((N_EXPERTS * cap, 1, D_MODEL), F32),
        compiler_params=pltpu.CompilerParams(dimension_semantics=("arbitrary", "arbitrary"),
                                             vmem_limit_bytes=VMEM_LIMIT, disable_bounds_checks=True),
        name="ec_ffn",
    )(idx, gate, xn, fw["w_gate"], fw["w_up"], fw["w_down"])


def _combine_kernel(offs_ref, idx_ref, x1_ref, y_hbm, o_ref, acc, ywin, yover, sem, osem,
                    *, cap, n_tiles, tile, width):
    j = pl.program_id(0)
    slot = j % 2
    total = N_EXPERTS * cap

    def window(t, e):
        first = e * cap + offs_ref[t * N_EXPERTS + e]
        return first, jnp.minimum(first, total - width)

    def win_copy(t, e, s):
        return pltpu.make_async_copy(y_hbm.at[pl.ds(window(t, e)[1], width)], ywin.at[s, e], sem.at[s])

    def fetch(t, s):
        for e in range(N_EXPERTS):
            win_copy(t, e, s).start()

    def add_rows(win, first, shift, count):
        def body(g, c):
            work = []
            for u in range(ROW_UNROLL):
                r = g * ROW_UNROLL + u
                tok = idx_ref[jnp.minimum(first + r, total - 1)] - j * tile
                dst = jnp.where(r < count, tok, tile + u)
                work.append((dst, acc[dst] + win[jnp.minimum(shift + r, width - 1)]))
            for dst, val in work:
                acc[dst] = val
            return c
        lax.fori_loop(0, (count + ROW_UNROLL - 1) // ROW_UNROLL, body, 0)

    @pl.when(j == 0)
    def _():
        fetch(0, 0)

    @pl.when(j + 1 < n_tiles)
    def _():
        fetch(j + 1, 1 - slot)

    acc[...] = jnp.zeros_like(acc)
    for e in range(N_EXPERTS):
        win_copy(j, e, slot).wait()

    for e in range(N_EXPERTS):
        first, start = window(j, e)
        count = offs_ref[(j + 1) * N_EXPERTS + e] - offs_ref[j * N_EXPERTS + e]
        add_rows(ywin.at[slot, e], first, first - start, jnp.minimum(count, width))

        def overflow(k, c, first=first, count=count):
            begin = first + k * width
            start = jnp.minimum(begin, total - width)
            cp = pltpu.make_async_copy(y_hbm.at[pl.ds(start, width)], yover, osem)
            cp.start()
            cp.wait()
            add_rows(yover, begin, begin - start, jnp.minimum(count - k * width, width))
            return c
        lax.fori_loop(1, (count + width - 1) // width, overflow, 0)

    o_ref[...] = x1_ref[...] + acc[0:tile].reshape(tile, D_MODEL)


def _combine(x1, y, idx, offs, cap, tile, width):
    n = x1.shape[0]
    n_tiles = n // tile
    grid_spec = pltpu.PrefetchScalarGridSpec(
        num_scalar_prefetch=2,
        grid=(n_tiles,),
        in_specs=[pl.BlockSpec((tile, D_MODEL), lambda i, offs, idx: (i, 0)),
                  pl.BlockSpec(memory_space=pl.ANY)],
        out_specs=pl.BlockSpec((tile, D_MODEL), lambda i, offs, idx: (i, 0)),
        scratch_shapes=[pltpu.VMEM((tile + ROW_UNROLL, 1, D_MODEL), F32),
                        pltpu.VMEM((2, N_EXPERTS, width, 1, D_MODEL), F32),
                        pltpu.VMEM((width, 1, D_MODEL), F32),
                        pltpu.SemaphoreType.DMA((2,)), pltpu.SemaphoreType.DMA(())],
    )
    return pl.pallas_call(
        functools.partial(_combine_kernel, cap=cap, n_tiles=n_tiles, tile=tile, width=width),
        grid_spec=grid_spec,
        out_shape=jax.ShapeDtypeStruct((n, D_MODEL), F32),
        compiler_params=_params("arbitrary"),
        name="ec_combine",
    )(offs, idx, x1, y)


def _ec_ffn(x1, xn, aff_t, fw):
    n = x1.shape[0]
    cap = EC_CAPACITY_FACTOR * n // N_EXPERTS
    chunks = cap // SEL_BLOCK
    step_tokens = min(SEL_STEP, n)
    thr, need = _threshold(aff_t, cap)
    slots, offs_t = _compact(aff_t, thr, need, chunks, step_tokens)
    idx = slots[:N_EXPERTS * chunks, 0, :].reshape(-1).astype(jnp.int32)
    gate = slots[:N_EXPERTS * chunks, 1, :].reshape(-1, 1)
    offs = jnp.concatenate([offs_t[:, :, 0], jnp.full((1, N_EXPERTS), cap, jnp.int32)], axis=0).reshape(-1)
    y = _ffn(xn, idx, gate, fw, cap, min(FFN_ROWS, cap // 2))
    return _combine(x1, y, idx, offs, cap, step_tokens, min(COMBINE_W, cap))


def _final_norm_kernel(x_ref, g_ref, o_ref):
    o_ref[...] = _rms(x_ref[...], g_ref[...])


def _final_norm(x, g, tile):
    n = x.shape[0]
    return pl.pallas_call(
        _final_norm_kernel,
        grid=(n // tile,),
        in_specs=[pl.BlockSpec((tile, D_MODEL), lambda i: (i, 0)),
                  pl.BlockSpec((1, D_MODEL), lambda i: (0, 0))],
        out_specs=pl.BlockSpec((tile, D_MODEL), lambda i: (i, 0)),
        out_shape=jax.ShapeDtypeStruct((n, D_MODEL), F32),
        compiler_params=_params("parallel"),
        name="final_norm",
    )(x, g)


def _pad_heads(w, n_heads, width):
    lead = w.shape[:-1]
    w = w.reshape(lead + (n_heads, width))
    w = jnp.pad(w, [(0, 0)] * len(lead) + [(0, 0), (0, LANE - width)])
    return w.reshape(lead + (n_heads * LANE,))


def _rope_tables(seq):
    rows = seq // GRID_W
    row = jnp.repeat(jnp.arange(rows, dtype=F32), GRID_W)
    col = jnp.tile(jnp.arange(GRID_W, dtype=F32), rows)

    def cos_sin(rot_dim):
        n_pair = rot_dim // 4
        freq = ROPE_THETA ** (-jnp.arange(n_pair, dtype=F32) / n_pair)
        ang = jnp.concatenate([row[:, None] * freq, col[:, None] * freq], axis=-1)
        return jnp.cos(ang), jnp.sin(ang)

    def place(parts):
        out = jnp.zeros((seq, LANE), F32)
        for start, val in parts:
            out = out.at[:, start:start + val.shape[1]].set(val)
        return out

    ca, sa = cos_sin(A_HEAD_DIM)
    cb, sb = cos_sin(B_ROPE)
    ha, hb = A_HEAD_DIM // 2, B_ROPE // 2
    ones = jnp.ones((seq, B_NOPE), F32)
    return (place([(0, ca), (ha, ca)]), place([(0, -sa)]), place([(ha, sa)]),
            place([(0, ones), (B_NOPE, cb), (B_NOPE + hb, cb)]), place([(B_NOPE, -sb)]),
            place([(B_NOPE + hb, sb)]))


def _prep_attn_layer(i, p, rope):
    w_in = p["w_in_attn"][i]
    o = 0
    w_qa = w_in[:, o:o + A_Q_W]; o += A_Q_W
    w_ka = w_in[:, o:o + A_KV_W]; o += A_KV_W
    w_va = w_in[:, o:o + A_KV_W]; o += A_KV_W
    w_cq = w_in[:, o:o + B_Q_RANK]; o += B_Q_RANK
    w_ckv = w_in[:, o:o + B_KV_RANK]; o += B_KV_RANK
    w_kr = w_in[:, o:o + B_ROPE]
    w_kr = jnp.pad(w_kr, [(0, 0), (B_NOPE, LANE - B_NOPE - B_ROPE)])
    w_all = jnp.concatenate([_pad_heads(w_qa, A_HEADS, A_HEAD_DIM), _pad_heads(w_ka, A_KV_HEADS, A_HEAD_DIM),
                             _pad_heads(w_va, A_KV_HEADS, A_HEAD_DIM), w_cq, w_ckv, w_kr], axis=1)
    w_ukv = p["w_ukv"][i].reshape(B_KV_RANK, B_HEADS, B_NOPE + B_V)
    w_uk = _pad_heads(w_ukv[:, :, :B_NOPE].reshape(B_KV_RANK, -1), B_HEADS, B_NOPE)
    w_uv = _pad_heads(w_ukv[:, :, B_NOPE:].reshape(B_KV_RANK, -1), B_HEADS, B_V)
    w_out = p["w_out_attn"][i].reshape(N_HEADS, A_HEAD_DIM, D_MODEL)
    w_out = jnp.pad(w_out, [(0, 0), (0, LANE - A_HEAD_DIM), (0, 0)]).reshape(N_HEADS * LANE, D_MODEL)
    return {
        "g": p["attn_norm"][i][None, :],
        "w_all": w_all.astype(BF16),
        "gq": jnp.tile(jnp.pad(p["qk_norm_q"][i], (0, LANE - A_HEAD_DIM)), A_HEADS)[None, :],
        "gk": jnp.tile(jnp.pad(p["qk_norm_k"][i], (0, LANE - A_HEAD_DIM)), A_KV_HEADS)[None, :],
        "mq": p["mla_q_norm"][i][None, :],
        "wuq": _pad_heads(p["w_uq"][i], B_HEADS, B_QK).astype(BF16),
        "mkv": p["mla_kv_norm"][i][None, :],
        "wukv": jnp.concatenate([w_uk, w_uv], axis=1).astype(BF16),
        "w_out": w_out.astype(BF16),
        "rope": rope,
    }


def _prep_sgu_layer(i, p):
    b_s = p["b_spatial"][i]
    return {
        "g": p["sgu_norm"][i][None, :],
        "w_in": p["w_in_sgu"][i].astype(BF16),
        "ln_g": p["sgu_ln_g"][i][None, :],
        "ln_b": p["sgu_ln_b"][i][None, :],
        "w_s": p["w_spatial"][i].astype(BF16),
        "b_s": jnp.repeat(b_s.T, C_GROUP_W, axis=1),
        "w_out": p["w_out_sgu"][i].astype(BF16),
    }


def _prep_ffn_layer(l, p, w_gate, w_up, w_down):
    return {
        "g": p["ffn_norm"][l][None, :],
        "wrt": p["w_router"][l].T.astype(BF16),
        "w_gate": w_gate[l], "w_up": w_up[l], "w_down": w_down[l],
    }


def _pick(n, target):
    t = min(n, target)
    while n % t:
        t //= 2
    return t


def _trunk(x, attn_layers, sgu_layers, ffn_layers, final_g):
    batch, seq, _ = x.shape
    n = batch * seq
    x = x.reshape(n, D_MODEL)
    tile = _pick(seq, 512)
    tq = _pick(seq, 512)
    for l, fw in enumerate(ffn_layers):
        if l % 2 == 0:
            lw = attn_layers[l // 2]
            q, k, v = _attn_proj(x, lw, seq, tile)
            o = _attention(q, k, v, batch, seq, tq)
            x1, xn, aff_t = _attn_out(o, x, lw, fw, tile)
        else:
            x1, xn, aff_t = _sgu(x, sgu_layers[l // 2], fw, tile)
        x = _ec_ffn(x1, xn, aff_t, fw)
    return _final_norm(x, final_g[None, :], tile).reshape(batch, seq, D_MODEL)


def kernel(x_prompt, x_sample, attn_norm, w_in_attn, qk_norm_q, qk_norm_k, mla_q_norm, w_uq, mla_kv_norm, w_ukv, w_out_attn, sgu_norm, w_in_sgu, sgu_ln_g, sgu_ln_b, w_spatial, b_spatial, w_out_sgu, ffn_norm, w_router, w_gate, w_up, w_down, final_norm):
    p = dict(attn_norm=attn_norm, w_in_attn=w_in_attn, qk_norm_q=qk_norm_q, qk_norm_k=qk_norm_k,
             mla_q_norm=mla_q_norm, w_uq=w_uq, mla_kv_norm=mla_kv_norm, w_ukv=w_ukv,
             w_out_attn=w_out_attn, sgu_norm=sgu_norm, w_in_sgu=w_in_sgu, sgu_ln_g=sgu_ln_g,
             sgu_ln_b=sgu_ln_b, w_spatial=w_spatial, b_spatial=b_spatial, w_out_sgu=w_out_sgu,
             ffn_norm=ffn_norm, w_router=w_router)
    depth = ffn_norm.shape[0]
    wg, wu, wd = w_gate.astype(BF16), w_up.astype(BF16), w_down.astype(BF16)
    ffn_layers = [_prep_ffn_layer(l, p, wg, wu, wd) for l in range(depth)]
    sgu_layers = [_prep_sgu_layer(i, p) for i in range(sgu_norm.shape[0])]
    outs = []
    for x in (x_prompt, x_sample):
        rope = _rope_tables(x.shape[1])
        attn_layers = [_prep_attn_layer(i, p, rope) for i in range(attn_norm.shape[0])]
        outs.append(_trunk(x, attn_layers, sgu_layers, ffn_layers, final_norm))
    return tuple(outs)
```

```python
import functools

import jax
import jax.numpy as jnp
from jax import lax
from jax.experimental import pallas as pl
from jax.experimental.pallas import tpu as pltpu

D_MODEL = 1024
GRID_W = 64
ROPE_THETA = 10000.0
EPS = 1e-6

A_HEADS = 8
A_KV_HEADS = 2
A_GROUP = A_HEADS // A_KV_HEADS
A_HEAD_DIM = 64
A_Q_W = A_HEADS * A_HEAD_DIM
A_KV_W = A_KV_HEADS * A_HEAD_DIM

B_HEADS = 8
B_Q_RANK = 384
B_KV_RANK = 256
B_NOPE = 64
B_ROPE = 32
B_V = 64
B_QK = B_NOPE + B_ROPE

N_HEADS = A_HEADS + B_HEADS
HEADS_PER_STEP = 4
LOG2_E = 1.4426950408889634

C_WIDTH = D_MODEL
C_GROUPS = 8
C_GROUP_W = C_WIDTH // C_GROUPS
C_CHUNK = 128

N_EXPERTS = 16
EXPERT_FF = 2 * D_MODEL
EC_CAPACITY_FACTOR = 2

LANE = 128
VMEM_LIMIT = 56 * 1024 * 1024
SEL_BLOCK = LANE
SEL_STEP = 512
FFN_ROWS = 512
COMBINE_W = 96
ROW_UNROLL = 8

BF16 = jnp.bfloat16
F32 = jnp.float32


def _params(*sem):
    return pltpu.CompilerParams(dimension_semantics=sem, vmem_limit_bytes=VMEM_LIMIT)


def _dot(a, b):
    return jnp.dot(a.astype(BF16), b.astype(BF16), preferred_element_type=F32)


def _dot_nt(a, b):
    return lax.dot_general(a.astype(BF16), b.astype(BF16), (((1,), (1,)), ((), ())),
                           preferred_element_type=F32)


def _rms(x, g):
    return x * lax.rsqrt(jnp.mean(x * x, axis=-1, keepdims=True) + EPS) * g


def _router_tail(x1, g_ref, wrt_ref, xn_ref, aff_ref):
    xn = _rms(x1, g_ref[...])
    xn_ref[...] = xn
    logits = _dot_nt(wrt_ref[...], xn)
    m = jnp.max(logits, axis=0, keepdims=True)
    e = jnp.exp(logits - m)
    aff_ref[...] = e / jnp.sum(e, axis=0, keepdims=True)


def _head_norm(z, n_heads, width):
    outs = []
    for h in range(n_heads):
        blk = z[:, h * LANE:(h + 1) * LANE]
        ms = jnp.sum(blk * blk, axis=-1, keepdims=True) * (1.0 / width)
        outs.append(blk * lax.rsqrt(ms + EPS))
    return jnp.concatenate(outs, axis=1)


def _rope(x, c, s_lo, s_hi, half, n_heads):
    w = x.shape[1]
    c = jnp.tile(c, (1, n_heads))
    s_lo = jnp.tile(s_lo, (1, n_heads))
    s_hi = jnp.tile(s_hi, (1, n_heads))
    return x * c + pltpu.roll(x, w - half, 1) * s_lo + pltpu.roll(x, half, 1) * s_hi


def _attn_proj_kernel(x_ref, g_ref, w_ref, gq_ref, gk_ref, mq_ref, wuq_ref, mkv_ref, wukv_ref,
                      ca_ref, sa1_ref, sa2_ref, cb_ref, sb1_ref, sb2_ref,
                      q_ref, k_ref, v_ref):
    h = _rms(x_ref[...], g_ref[...])
    z = _dot(h, w_ref[...])
    o = 0
    qa = z[:, o:o + A_HEADS * LANE]; o += A_HEADS * LANE
    ka = z[:, o:o + A_KV_HEADS * LANE]; o += A_KV_HEADS * LANE
    va = z[:, o:o + A_KV_HEADS * LANE]; o += A_KV_HEADS * LANE
    cq = z[:, o:o + B_Q_RANK]; o += B_Q_RANK
    ckv = z[:, o:o + B_KV_RANK]; o += B_KV_RANK
    kr = z[:, o:o + LANE]

    ca, sa1, sa2 = ca_ref[...], sa1_ref[...], sa2_ref[...]
    cb, sb1, sb2 = cb_ref[...], sb1_ref[...], sb2_ref[...]

    qa = _rope(_head_norm(qa, A_HEADS, A_HEAD_DIM) * gq_ref[...], ca, sa1, sa2, A_HEAD_DIM // 2, A_HEADS)
    ka = _rope(_head_norm(ka, A_KV_HEADS, A_HEAD_DIM) * gk_ref[...], ca, sa1, sa2, A_HEAD_DIM // 2,
               A_KV_HEADS)

    qb = _rope(_dot(_rms(cq, mq_ref[...]), wuq_ref[...]), cb, sb1, sb2, B_ROPE // 2, B_HEADS)
    kvb = _dot(_rms(ckv, mkv_ref[...]), wukv_ref[...])
    kr = _rope(kr, cb, sb1, sb2, B_ROPE // 2, 1)
    kb = kvb[:, :B_HEADS * LANE] + jnp.tile(kr, (1, B_HEADS))
    vb = kvb[:, B_HEADS * LANE:]

    q_ref[...] = jnp.concatenate([qa, qb], axis=1).astype(BF16)
    per_query_head = lambda x: [x[:, g * LANE:(g + 1) * LANE] for g in range(A_KV_HEADS)
                                for _ in range(A_GROUP)]
    k_ref[...] = jnp.concatenate(per_query_head(ka) + [kb], axis=1).astype(BF16)
    v_ref[...] = jnp.concatenate(per_query_head(va) + [vb], axis=1).astype(BF16)


def _attn_proj(x, lw, seq, tile):
    n = x.shape[0]
    nt = n // tile
    per_seq = seq // tile
    full = lambda a: pl.BlockSpec(a.shape, lambda i: (0,) * a.ndim)
    tok = lambda w: pl.BlockSpec((tile, w), lambda i: (i, 0))
    tab = pl.BlockSpec((tile, LANE), lambda i: (i % per_seq, 0))
    weights = (lw["g"], lw["w_all"], lw["gq"], lw["gk"], lw["mq"], lw["wuq"], lw["mkv"], lw["wukv"])
    return pl.pallas_call(
        _attn_proj_kernel,
        grid=(nt,),
        in_specs=[tok(D_MODEL)] + [full(a) for a in weights] + [tab] * 6,
        out_specs=[tok(N_HEADS * LANE)] * 3,
        out_shape=[jax.ShapeDtypeStruct((n, N_HEADS * LANE), BF16)] * 3,
        compiler_params=_params("parallel"),
        name="attn_proj",
    )(x, *weights, *lw["rope"])


def _attn_kernel(q_ref, k_ref, v_ref, o_ref):
    pair = pl.program_id(1)
    scale = jnp.where(pair < A_HEADS // HEADS_PER_STEP, A_HEAD_DIM ** -0.5, B_QK ** -0.5).astype(F32)
    c = scale * LOG2_E
    heads = [slice(h * LANE, (h + 1) * LANE) for h in range(HEADS_PER_STEP)]
    scores = [_dot_nt(q_ref[:, sl], k_ref[:, sl]) for sl in heads]
    outs = []
    for s, sl in zip(scores, heads):
        m = jnp.max(s, axis=-1, keepdims=True)
        p = jnp.exp2((s - m) * c)
        l = jnp.sum(p, axis=-1, keepdims=True)
        outs.append((_dot(p, v_ref[:, sl]) / l)[:, :A_HEAD_DIM])
    o_ref[...] = jnp.concatenate(outs, axis=1).astype(BF16)


def _attention(q, k, v, batch, seq, tq):
    n = q.shape[0]
    nq = seq // tq
    width = HEADS_PER_STEP * LANE
    return pl.pallas_call(
        _attn_kernel,
        grid=(batch, N_HEADS // HEADS_PER_STEP, nq),
        in_specs=[pl.BlockSpec((tq, width), lambda b, h, i: (b * nq + i, h)),
                  pl.BlockSpec((seq, width), lambda b, h, i: (b, h)),
                  pl.BlockSpec((seq, width), lambda b, h, i: (b, h))],
        out_specs=pl.BlockSpec((tq, HEADS_PER_STEP * A_HEAD_DIM), lambda b, h, i: (b * nq + i, h)),
        out_shape=jax.ShapeDtypeStruct((n, N_HEADS * A_HEAD_DIM), BF16),
        compiler_params=_params("parallel", "parallel", "parallel"),
        name="attention",
    )(q, k, v)


def _attn_out_kernel(o_ref, x_ref, w_ref, g_ref, wrt_ref, x1_ref, xn_ref, aff_ref):
    x1 = x_ref[...] + _dot(o_ref[...], w_ref[...])
    x1_ref[...] = x1
    _router_tail(x1, g_ref, wrt_ref, xn_ref, aff_ref)


def _tail_specs(n, tile):
    out_specs = [pl.BlockSpec((tile, D_MODEL), lambda i: (i, 0)),
                 pl.BlockSpec((tile, D_MODEL), lambda i: (i, 0)),
                 pl.BlockSpec((N_EXPERTS, tile), lambda i: (0, i))]
    out_shape = [jax.ShapeDtypeStruct((n, D_MODEL), F32),
                 jax.ShapeDtypeStruct((n, D_MODEL), F32),
                 jax.ShapeDtypeStruct((N_EXPERTS, n), F32)]
    return out_specs, out_shape


def _attn_out(o, x, lw, fw, tile):
    n = x.shape[0]
    full = lambda a: pl.BlockSpec(a.shape, lambda i: (0,) * a.ndim)
    out_specs, out_shape = _tail_specs(n, tile)
    weights = (lw["w_out"], fw["g"], fw["wrt"])
    return pl.pallas_call(
        _attn_out_kernel,
        grid=(n // tile,),
        in_specs=[pl.BlockSpec((tile, N_HEADS * A_HEAD_DIM), lambda i: (i, 0)),
                  pl.BlockSpec((tile, D_MODEL), lambda i: (i, 0))] + [full(a) for a in weights],
        out_specs=out_specs,
        out_shape=out_shape,
        compiler_params=_params("parallel"),
        name="attn_out",
    )(o, x, *weights)


def _sgu_kernel(x_ref, g_ref, win_ref, lng_ref, lnb_ref, ws_ref, bs_ref, wout_ref, gf_ref, wrt_ref,
                x1_ref, xn_ref, aff_ref):
    x = x_ref[...]
    tile = x.shape[0]
    z = jax.nn.gelu(_dot(_rms(x, g_ref[...]), win_ref[...]))
    u, v = z[:, :C_WIDTH], z[:, C_WIDTH:]
    mu = jnp.mean(v, axis=-1, keepdims=True)
    vc = v - mu
    v = vc * lax.rsqrt(jnp.mean(vc * vc, axis=-1, keepdims=True) + EPS) * lng_ref[...] + lnb_ref[...]
    v = v.astype(BF16)
    bias = bs_ref[...]
    rows = []
    for c in range(tile // C_CHUNK):
        vch = v[c * C_CHUNK:(c + 1) * C_CHUNK, :]
        cols = [_dot(ws_ref[g], vch[:, g * C_GROUP_W:(g + 1) * C_GROUP_W]) for g in range(C_GROUPS)]
        rows.append(jnp.concatenate(cols, axis=1) + bias)
    s = jnp.concatenate(rows, axis=0)
    x1 = x + _dot(u * s, wout_ref[...])
    x1_ref[...] = x1
    _router_tail(x1, gf_ref, wrt_ref, xn_ref, aff_ref)


def _sgu(x, lw, fw, tile):
    n = x.shape[0]
    full = lambda a: pl.BlockSpec(a.shape, lambda i: (0,) * a.ndim)
    out_specs, out_shape = _tail_specs(n, tile)
    weights = (lw["g"], lw["w_in"], lw["ln_g"], lw["ln_b"], lw["w_s"], lw["b_s"], lw["w_out"],
               fw["g"], fw["wrt"])
    return pl.pallas_call(
        _sgu_kernel,
        grid=(n // tile,),
        in_specs=[pl.BlockSpec((tile, D_MODEL), lambda i: (i, 0))] + [full(a) for a in weights],
        out_specs=out_specs,
        out_shape=out_shape,
        compiler_params=_params("parallel"),
        name="sgu",
    )(x, *weights)


def _threshold_kernel(aff_ref, thr_ref, need_ref, *, cap):
    def count(pred):
        return jnp.sum(jnp.where(pred, 1.0, 0.0), axis=1, keepdims=True)

    def body(i, cur):
        cand = cur | jnp.left_shift(jnp.int32(1), 30 - i)
        bits = lax.bitcast_convert_type(aff_ref[...], jnp.int32)
        return jnp.where(count(bits >= cand) >= cap, cand, cur)

    thr = lax.fori_loop(0, 31, body, jnp.zeros((N_EXPERTS, 1), jnp.int32))
    bits = lax.bitcast_convert_type(aff_ref[...], jnp.int32)
    thr_ref[...] = thr
    need_ref[...] = cap - count(bits > thr)


def _threshold(aff_t, cap):
    n = aff_t.shape[1]
    small = lambda dt: (pl.BlockSpec((N_EXPERTS, 1), lambda i: (0, 0)),
                        jax.ShapeDtypeStruct((N_EXPERTS, 1), dt))
    (s0, o0), (s1, o1) = small(jnp.int32), small(F32)
    return pl.pallas_call(
        functools.partial(_threshold_kernel, cap=cap),
        grid=(1,),
        in_specs=[pl.BlockSpec((N_EXPERTS, n), lambda i: (0, 0))],
        out_specs=[s0, s1],
        out_shape=[o0, o1],
        compiler_params=_params("arbitrary"),
        name="ec_threshold",
    )(aff_t)


def _compact_kernel(aff_ref, thr_ref, need_ref, slot_ref, offs_ref, carry_ref, tie_ref, *, chunks, step_tokens):
    step = pl.program_id(0)

    @pl.when(step == 0)
    def _():
        slot_ref[...] = jnp.zeros_like(slot_ref)
        carry_ref[...] = jnp.zeros_like(carry_ref)
        tie_ref[...] = jnp.zeros_like(tie_ref)

    offs_ref[0] = jnp.broadcast_to(carry_ref[...], (N_EXPERTS, LANE))

    row_i = lax.broadcasted_iota(jnp.int32, (SEL_BLOCK, SEL_BLOCK), 0)
    col_i = lax.broadcasted_iota(jnp.int32, (SEL_BLOCK, SEL_BLOCK), 1)
    upper = jnp.where(row_i < col_i, 1.0, 0.0).astype(BF16)
    lane = lax.broadcasted_iota(jnp.int32, (N_EXPERTS, SEL_BLOCK), 1)
    shifts = [1 << b for b in range(SEL_BLOCK.bit_length() - 1)]
    thr = thr_ref[...]
    need = need_ref[...]

    for sb in range(step_tokens // SEL_BLOCK):
        aff = aff_ref[:, sb * SEL_BLOCK:(sb + 1) * SEL_BLOCK]
        bits = lax.bitcast_convert_type(aff, jnp.int32)
        eq = bits == thr
        eq_f = jnp.where(eq, 1.0, 0.0)
        ties_before = jnp.dot(eq_f.astype(BF16), upper, preferred_element_type=F32) + tie_ref[...]
        sel = (bits > thr) | (eq & (ties_before < need))
        sel_f = jnp.where(sel, 1.0, 0.0)
        tie_ref[...] = tie_ref[...] + jnp.sum(eq_f, axis=1, keepdims=True)

        before = jnp.dot(sel_f.astype(BF16), upper, preferred_element_type=F32)
        count = jnp.sum(sel_f, axis=1, keepdims=True).astype(jnp.int32)
        carry = carry_ref[...]
        tok = (step * step_tokens + sb * SEL_BLOCK + lane).astype(F32)

        dist = jnp.where(sel, lane - before.astype(jnp.int32), 0)
        ids = jnp.where(sel, tok, 0.0)
        gates = jnp.where(sel, aff, 0.0)
        for sh in shifts:
            move = (dist & sh) != 0
            step_left = lambda x, zero: (jnp.where(move, zero, x)
                                         + pltpu.roll(jnp.where(move, x, zero), SEL_BLOCK - sh, 1))
            ids, gates, dist = step_left(ids, 0.0), step_left(gates, 0.0), step_left(dist, 0)

        off = carry & (SEL_BLOCK - 1)
        for sh in shifts:
            turn = (off & sh) != 0
            ids = jnp.where(turn, pltpu.roll(ids, sh, 1), ids)
            gates = jnp.where(turn, pltpu.roll(gates, sh, 1), gates)
        end = off + count
        in_lo = (lane >= off) & (lane < end)
        in_hi = lane < end - SEL_BLOCK
        pick = lambda m, x, e: jnp.where(m, x, 0.0)[e:e + 1, :]

        for e in range(N_EXPERTS):
            row = e * chunks + carry[e, 0] // SEL_BLOCK
            lo = jnp.concatenate([pick(in_lo, ids, e), pick(in_lo, gates, e)], axis=0)
            hi = jnp.concatenate([pick(in_hi, ids, e), pick(in_hi, gates, e)], axis=0)
            slot_ref[row] = slot_ref[row] + lo
            slot_ref[row + 1] = slot_ref[row + 1] + hi

        carry_ref[...] = carry + count


def _compact(aff_t, thr, need, chunks, step_tokens):
    n = aff_t.shape[1]
    n_steps = n // step_tokens
    n_rows = N_EXPERTS * chunks + 1
    small = lambda a: pl.BlockSpec(a.shape, lambda i: (0, 0))
    return pl.pallas_call(
        functools.partial(_compact_kernel, chunks=chunks, step_tokens=step_tokens),
        grid=(n_steps,),
        in_specs=[pl.BlockSpec((N_EXPERTS, step_tokens), lambda i: (0, i)), small(thr), small(need)],
        out_specs=[pl.BlockSpec((n_rows, 2, SEL_BLOCK), lambda i: (0, 0, 0)),
                   pl.BlockSpec((1, N_EXPERTS, LANE), lambda i: (i, 0, 0))],
        out_shape=[jax.ShapeDtypeStruct((n_rows, 2, SEL_BLOCK), F32),
                   jax.ShapeDtypeStruct((n_steps, N_EXPERTS, LANE), jnp.int32)],
        scratch_shapes=[pltpu.VMEM((N_EXPERTS, 1), jnp.int32), pltpu.VMEM((N_EXPERTS, 1), F32)],
        compiler_params=_params("arbitrary"),
        name="ec_compact",
    )(aff_t, thr, need)


def _ffn_kernel(idx_ref, gate_ref, xn_hbm, wg_ref, wu_ref, wd_ref, y_ref, buf0, buf1, sem, *, rows, n_pairs):
    pair = pl.program_id(0) * pl.num_programs(1) + pl.program_id(1)

    def row_copy(tok, buf, r, s):
        return pltpu.make_async_copy(xn_hbm.at[pl.ds(tok, 1), :], buf.at[pl.ds(r, 1), :], sem.at[s])

    def gather(chunk, buf, s):
        for r in range(rows):
            row_copy(idx_ref[chunk * rows + r], buf, r, s).start()

    def wait(buf, s):
        pltpu.make_async_copy(xn_hbm.at[pl.ds(0, rows), :], buf, sem.at[s]).wait()

    def compute(buf, half):
        xg = buf[...].astype(BF16)
        hid = jax.nn.silu(_dot(xg, wg_ref[0])) * _dot(xg, wu_ref[0])
        y = _dot(hid, wd_ref[0]) * gate_ref[half * rows:(half + 1) * rows, :]
        y_ref[half * rows:(half + 1) * rows] = y.reshape(rows, 1, D_MODEL)

    @pl.when(pair == 0)
    def _():
        def body(r, c):
            row_copy(idx_ref[r], buf0, r, 0).start()
            return c
        lax.fori_loop(0, rows, body, 0)

    wait(buf0, 0)
    gather(2 * pair + 1, buf1, 1)
    compute(buf0, 0)
    wait(buf1, 1)
    gather(jnp.where(pair + 1 < n_pairs, 2 * pair + 2, 0), buf0, 0)
    compute(buf1, 1)

    @pl.when(pair == n_pairs - 1)
    def _():
        wait(buf0, 0)


def _ffn(xn, idx, gate, fw, cap, rows):
    pairs = cap // (2 * rows)
    w_spec = lambda shape: pl.BlockSpec((1,) + shape, lambda e, c, idx: (e, 0, 0))
    grid_spec = pltpu.PrefetchScalarGridSpec(
        num_scalar_prefetch=1,
        grid=(N_EXPERTS, pairs),
        in_specs=[pl.BlockSpec((2 * rows, 1), lambda e, c, idx: (e * pairs + c, 0)),
                  pl.BlockSpec(memory_space=pl.ANY),
                  w_spec((D_MODEL, EXPERT_FF)), w_spec((D_MODEL, EXPERT_FF)), w_spec((EXPERT_FF, D_MODEL))],
        out_specs=pl.BlockSpec((2 * rows, 1, D_MODEL), lambda e, c, idx: (e * pairs + c, 0, 0)),
        scratch_shapes=[pltpu.VMEM((rows, D_MODEL), F32), pltpu.VMEM((rows, D_MODEL), F32),
                        pltpu.SemaphoreType.DMA((2,))],
    )
    return pl.pallas_call(
        functools.partial(_ffn_kernel, rows=rows, n_pairs=N_EXPERTS * pairs),
        grid_spec=grid_spec,
        out_shape=jax.ShapeDtypeStruct((N_EXPERTS * cap, 1, D_MODEL), F32),
        compiler_params=pltpu.CompilerParams(dimension_semantics=("arbitrary", "arbitrary"),
                                             vmem_limit_bytes=VMEM_LIMIT, disable_bounds_checks=True),
        name="ec_ffn",
    )(idx, gate, xn, fw["w_gate"], fw["w_up"], fw["w_down"])


def _combine_kernel(offs_ref, idx_ref, x1_ref, y_hbm, o_ref, acc, ywin, yover, sem, osem,
                    *, cap, n_tiles, tile, width):
    j = pl.program_id(0)
    slot = j % 2
    total = N_EXPERTS * cap

    def window(t, e):
        first = e * cap + offs_ref[t * N_EXPERTS + e]
        return first, jnp.minimum(first, total - width)

    def win_copy(t, e, s):
        return pltpu.make_async_copy(y_hbm.at[pl.ds(window(t, e)[1], width)], ywin.at[s, e], sem.at[s])

    def fetch(t, s):
        for e in range(N_EXPERTS):
            win_copy(t, e, s).start()

    def add_rows(win, first, shift, count):
        def group(g, partial):
            work = []
            for u in range(ROW_UNROLL):
                r = g * ROW_UNROLL + u
                if partial:
                    tok = idx_ref[jnp.minimum(first + r, total - 1)] - j * tile
                    dst = jnp.where(r < count, tok, tile + u)
                    src = jnp.minimum(shift + r, width - 1)
                else:
                    dst = idx_ref[first + r] - j * tile
                    src = shift + r
                work.append((dst, acc[dst] + win[src]))
            for dst, val in work:
                acc[dst] = val

        def body(g, c):
            group(g, False)
            return c
        whole = count // ROW_UNROLL
        lax.fori_loop(0, whole, body, 0)
        group(whole, True)

    @pl.when(j == 0)
    def _():
        fetch(0, 0)

    @pl.when(j + 1 < n_tiles)
    def _():
        fetch(j + 1, 1 - slot)

    acc[...] = jnp.zeros_like(acc)
    for e in range(N_EXPERTS):
        win_copy(j, e, slot).wait()

    for e in range(N_EXPERTS):
        first, start = window(j, e)
        count = offs_ref[(j + 1) * N_EXPERTS + e] - offs_ref[j * N_EXPERTS + e]
        add_rows(ywin.at[slot, e], first, first - start, jnp.minimum(count, width))

        def overflow(k, c, first=first, count=count):
            begin = first + k * width
            start = jnp.minimum(begin, total - width)
            cp = pltpu.make_async_copy(y_hbm.at[pl.ds(start, width)], yover, osem)
            cp.start()
            cp.wait()
            add_rows(yover, begin, begin - start, jnp.minimum(count - k * width, width))
            return c
        lax.fori_loop(1, (count + width - 1) // width, overflow, 0)

    o_ref[...] = x1_ref[...] + acc[0:tile].reshape(tile, D_MODEL)


def _combine(x1, y, idx, offs, cap, tile, width):
    n = x1.shape[0]
    n_tiles = n // tile
    grid_spec = pltpu.PrefetchScalarGridSpec(
        num_scalar_prefetch=2,
        grid=(n_tiles,),
        in_specs=[pl.BlockSpec((tile, D_MODEL), lambda i, offs, idx: (i, 0)),
                  pl.BlockSpec(memory_space=pl.ANY)],
        out_specs=pl.BlockSpec((tile, D_MODEL), lambda i, offs, idx: (i, 0)),
        scratch_shapes=[pltpu.VMEM((tile + ROW_UNROLL, 1, D_MODEL), F32),
                        pltpu.VMEM((2, N_EXPERTS, width, 1, D_MODEL), F32),
                        pltpu.VMEM((width, 1, D_MODEL), F32),
                        pltpu.SemaphoreType.DMA((2,)), pltpu.SemaphoreType.DMA(())],
    )
    return pl.pallas_call(
        functools.partial(_combine_kernel, cap=cap, n_tiles=n_tiles, tile=tile, width=width),
        grid_spec=grid_spec,
        out_shape=jax.ShapeDtypeStruct((n, D_MODEL), F32),
        compiler_params=_params("arbitrary"),
        name="ec_combine",
    )(offs, idx, x1, y)


def _ec_ffn(x1, xn, aff_t, fw):
    n = x1.shape[0]
    cap = EC_CAPACITY_FACTOR * n // N_EXPERTS
    chunks = cap // SEL_BLOCK
    step_tokens = min(SEL_STEP, n)
    thr, need = _threshold(aff_t, cap)
    slots, offs_t = _compact(aff_t, thr, need, chunks, step_tokens)
    idx = slots[:N_EXPERTS * chunks, 0, :].reshape(-1).astype(jnp.int32)
    gate = slots[:N_EXPERTS * chunks, 1, :].reshape(-1, 1)
    offs = jnp.concatenate([offs_t[:, :, 0], jnp.full((1, N_EXPERTS), cap, jnp.int32)], axis=0).reshape(-1)
    y = _ffn(xn, idx, gate, fw, cap, min(FFN_ROWS, cap // 2))
    return _combine(x1, y, idx, offs, cap, step_tokens, min(COMBINE_W, cap))


def _final_norm_kernel(x_ref, g_ref, o_ref):
    o_ref[...] = _rms(x_ref[...], g_ref[...])


def _final_norm(x, g, tile):
    n = x.shape[0]
    return pl.pallas_call(
        _final_norm_kernel,
        grid=(n // tile,),
        in_specs=[pl.BlockSpec((tile, D_MODEL), lambda i: (i, 0)),
                  pl.BlockSpec((1, D_MODEL), lambda i: (0, 0))],
        out_specs=pl.BlockSpec((tile, D_MODEL), lambda i: (i, 0)),
        out_shape=jax.ShapeDtypeStruct((n, D_MODEL), F32),
        compiler_params=_params("parallel"),
        name="final_norm",
    )(x, g)


def _pad_heads(w, n_heads, width):
    lead = w.shape[:-1]
    w = w.reshape(lead + (n_heads, width))
    w = jnp.pad(w, [(0, 0)] * len(lead) + [(0, 0), (0, LANE - width)])
    return w.reshape(lead + (n_heads * LANE,))


def _rope_tables(seq):
    rows = seq // GRID_W
    row = jnp.repeat(jnp.arange(rows, dtype=F32), GRID_W)
    col = jnp.tile(jnp.arange(GRID_W, dtype=F32), rows)

    def cos_sin(rot_dim):
        n_pair = rot_dim // 4
        freq = ROPE_THETA ** (-jnp.arange(n_pair, dtype=F32) / n_pair)
        ang = jnp.concatenate([row[:, None] * freq, col[:, None] * freq], axis=-1)
        return jnp.cos(ang), jnp.sin(ang)

    def place(parts):
        out = jnp.zeros((seq, LANE), F32)
        for start, val in parts:
            out = out.at[:, start:start + val.shape[1]].set(val)
        return out

    ca, sa = cos_sin(A_HEAD_DIM)
    cb, sb = cos_sin(B_ROPE)
    ha, hb = A_HEAD_DIM // 2, B_ROPE // 2
    ones = jnp.ones((seq, B_NOPE), F32)
    return (place([(0, ca), (ha, ca)]), place([(0, -sa)]), place([(ha, sa)]),
            place([(0, ones), (B_NOPE, cb), (B_NOPE + hb, cb)]), place([(B_NOPE, -sb)]),
            place([(B_NOPE + hb, sb)]))


def _prep_attn_layer(i, p, rope):
    w_in = p["w_in_attn"][i]
    o = 0
    w_qa = w_in[:, o:o + A_Q_W]; o += A_Q_W
    w_ka = w_in[:, o:o + A_KV_W]; o += A_KV_W
    w_va = w_in[:, o:o + A_KV_W]; o += A_KV_W
    w_cq = w_in[:, o:o + B_Q_RANK]; o += B_Q_RANK
    w_ckv = w_in[:, o:o + B_KV_RANK]; o += B_KV_RANK
    w_kr = w_in[:, o:o + B_ROPE]
    w_kr = jnp.pad(w_kr, [(0, 0), (B_NOPE, LANE - B_NOPE - B_ROPE)])
    w_all = jnp.concatenate([_pad_heads(w_qa, A_HEADS, A_HEAD_DIM), _pad_heads(w_ka, A_KV_HEADS, A_HEAD_DIM),
                             _pad_heads(w_va, A_KV_HEADS, A_HEAD_DIM), w_cq, w_ckv, w_kr], axis=1)
    w_ukv = p["w_ukv"][i].reshape(B_KV_RANK, B_HEADS, B_NOPE + B_V)
    w_uk = _pad_heads(w_ukv[:, :, :B_NOPE].reshape(B_KV_RANK, -1), B_HEADS, B_NOPE)
    w_uv = _pad_heads(w_ukv[:, :, B_NOPE:].reshape(B_KV_RANK, -1), B_HEADS, B_V)
    return {
        "g": p["attn_norm"][i][None, :],
        "w_all": w_all.astype(BF16),
        "gq": jnp.tile(jnp.pad(p["qk_norm_q"][i], (0, LANE - A_HEAD_DIM)), A_HEADS)[None, :],
        "gk": jnp.tile(jnp.pad(p["qk_norm_k"][i], (0, LANE - A_HEAD_DIM)), A_KV_HEADS)[None, :],
        "mq": p["mla_q_norm"][i][None, :],
        "wuq": _pad_heads(p["w_uq"][i], B_HEADS, B_QK).astype(BF16),
        "mkv": p["mla_kv_norm"][i][None, :],
        "wukv": jnp.concatenate([w_uk, w_uv], axis=1).astype(BF16),
        "w_out": p["w_out_attn"][i].astype(BF16),
        "rope": rope,
    }


def _prep_sgu_layer(i, p):
    b_s = p["b_spatial"][i]
    return {
        "g": p["sgu_norm"][i][None, :],
        "w_in": p["w_in_sgu"][i].astype(BF16),
        "ln_g": p["sgu_ln_g"][i][None, :],
        "ln_b": p["sgu_ln_b"][i][None, :],
        "w_s": p["w_spatial"][i].astype(BF16),
        "b_s": jnp.repeat(b_s.T, C_GROUP_W, axis=1),
        "w_out": p["w_out_sgu"][i].astype(BF16),
    }


def _prep_ffn_layer(l, p, w_gate, w_up, w_down):
    return {
        "g": p["ffn_norm"][l][None, :],
        "wrt": p["w_router"][l].T.astype(BF16),
        "w_gate": w_gate[l], "w_up": w_up[l], "w_down": w_down[l],
    }


def _pick(n, target):
    t = min(n, target)
    while n % t:
        t //= 2
    return t


def _trunk(x, attn_layers, sgu_layers, ffn_layers, final_g):
    batch, seq, _ = x.shape
    n = batch * seq
    x = x.reshape(n, D_MODEL)
    tile = _pick(seq, 512)
    tq = _pick(seq, 256)
    for l, fw in enumerate(ffn_layers):
        if l % 2 == 0:
            lw = attn_layers[l // 2]
            q, k, v = _attn_proj(x, lw, seq, tile)
            o = _attention(q, k, v, batch, seq, tq)
            x1, xn, aff_t = _attn_out(o, x, lw, fw, tile)
        else:
            x1, xn, aff_t = _sgu(x, sgu_layers[l // 2], fw, tile)
        x = _ec_ffn(x1, xn, aff_t, fw)
    return _final_norm(x, final_g[None, :], tile).reshape(batch, seq, D_MODEL)


def kernel(x_prompt, x_sample, attn_norm, w_in_attn, qk_norm_q, qk_norm_k, mla_q_norm, w_uq, mla_kv_norm, w_ukv, w_out_attn, sgu_norm, w_in_sgu, sgu_ln_g, sgu_ln_b, w_spatial, b_spatial, w_out_sgu, ffn_norm, w_router, w_gate, w_up, w_down, final_norm):
    p = dict(attn_norm=attn_norm, w_in_attn=w_in_attn, qk_norm_q=qk_norm_q, qk_norm_k=qk_norm_k,
             mla_q_norm=mla_q_norm, w_uq=w_uq, mla_kv_norm=mla_kv_norm, w_ukv=w_ukv,
             w_out_attn=w_out_attn, sgu_norm=sgu_norm, w_in_sgu=w_in_sgu, sgu_ln_g=sgu_ln_g,
             sgu_ln_b=sgu_ln_b, w_spatial=w_spatial, b_spatial=b_spatial, w_out_sgu=w_out_sgu,
             ffn_norm=ffn_norm, w_router=w_router)
    depth = ffn_norm.shape[0]
    wg, wu, wd = w_gate.astype(BF16), w_up.astype(BF16), w_down.astype(BF16)
    ffn_layers = [_prep_ffn_layer(l, p, wg, wu, wd) for l in range(depth)]
    sgu_layers = [_prep_sgu_layer(i, p) for i in range(sgu_norm.shape[0])]
    outs = []
    for x in (x_prompt, x_sample):
        rope = _rope_tables(x.shape[1])
        attn_layers = [_prep_attn_layer(i, p, rope) for i in range(attn_norm.shape[0])]
        outs.append(_trunk(x, attn_layers, sgu_layers, ffn_layers, final_norm))
    return tuple(outs)
```

```python
import functools

import jax
import jax.numpy as jnp
from jax import lax
from jax.experimental import pallas as pl
from jax.experimental.pallas import tpu as pltpu

D_MODEL = 1024
GRID_W = 64
ROPE_THETA = 10000.0
EPS = 1e-6

A_HEADS = 8
A_KV_HEADS = 2
A_GROUP = A_HEADS // A_KV_HEADS
A_HEAD_DIM = 64
A_Q_W = A_HEADS * A_HEAD_DIM
A_KV_W = A_KV_HEADS * A_HEAD_DIM

B_HEADS = 8
B_Q_RANK = 384
B_KV_RANK = 256
B_NOPE = 64
B_ROPE = 32
B_V = 64
B_QK = B_NOPE + B_ROPE

N_HEADS = A_HEADS + B_HEADS
HEADS_PER_STEP = 4
LOG2_E = 1.4426950408889634

C_WIDTH = D_MODEL
C_GROUPS = 8
C_GROUP_W = C_WIDTH // C_GROUPS
C_CHUNK = 128

N_EXPERTS = 16
EXPERT_FF = 2 * D_MODEL
EC_CAPACITY_FACTOR = 2

LANE = 128
VMEM_LIMIT = 56 * 1024 * 1024
SEL_BLOCK = LANE
SEL_STEP = 1024
FFN_ROWS = 512
COMBINE_W = 176
ROW_UNROLL = 8
BF16 = jnp.bfloat16
F32 = jnp.float32


def _params(*sem):
    return pltpu.CompilerParams(dimension_semantics=sem, vmem_limit_bytes=VMEM_LIMIT)


def _dot(a, b):
    return jnp.dot(a.astype(BF16), b.astype(BF16), preferred_element_type=F32)


def _dot_nt(a, b):
    return lax.dot_general(a.astype(BF16), b.astype(BF16), (((1,), (1,)), ((), ())),
                           preferred_element_type=F32)


def _rms(x, g):
    return x * lax.rsqrt(jnp.mean(x * x, axis=-1, keepdims=True) + EPS) * g


def _router_tail(x1, g_ref, wrt_ref, xn_ref, aff_ref):
    xn = _rms(x1, g_ref[...])
    xn_ref[...] = xn
    logits = _dot_nt(wrt_ref[...], xn)
    m = jnp.max(logits, axis=0, keepdims=True)
    e = jnp.exp(logits - m)
    aff_ref[...] = e / jnp.sum(e, axis=0, keepdims=True)


def _head_norm(z, n_heads, width):
    outs = []
    for h in range(n_heads):
        blk = z[:, h * LANE:(h + 1) * LANE]
        ms = jnp.sum(blk * blk, axis=-1, keepdims=True) * (1.0 / width)
        outs.append(blk * lax.rsqrt(ms + EPS))
    return jnp.concatenate(outs, axis=1)


def _rope(x, c, s_lo, s_hi, half, n_heads):
    w = x.shape[1]
    c = jnp.tile(c, (1, n_heads))
    s_lo = jnp.tile(s_lo, (1, n_heads))
    s_hi = jnp.tile(s_hi, (1, n_heads))
    return x * c + pltpu.roll(x, w - half, 1) * s_lo + pltpu.roll(x, half, 1) * s_hi


def _attn_proj_kernel(x_ref, g_ref, w_ref, gq_ref, gk_ref, mq_ref, wuq_ref, mkv_ref, wukv_ref,
                      ca_ref, sa1_ref, sa2_ref, cb_ref, sb1_ref, sb2_ref,
                      q_ref, k_ref, v_ref):
    h = _rms(x_ref[...], g_ref[...])
    z = _dot(h, w_ref[...])
    o = 0
    qa = z[:, o:o + A_HEADS * LANE]; o += A_HEADS * LANE
    ka = z[:, o:o + A_KV_HEADS * LANE]; o += A_KV_HEADS * LANE
    va = z[:, o:o + A_KV_HEADS * LANE]; o += A_KV_HEADS * LANE
    cq = z[:, o:o + B_Q_RANK]; o += B_Q_RANK
    ckv = z[:, o:o + B_KV_RANK]; o += B_KV_RANK
    kr = z[:, o:o + LANE]

    ca, sa1, sa2 = ca_ref[...], sa1_ref[...], sa2_ref[...]
    cb, sb1, sb2 = cb_ref[...], sb1_ref[...], sb2_ref[...]

    qa = _rope(_head_norm(qa, A_HEADS, A_HEAD_DIM) * gq_ref[...], ca, sa1, sa2, A_HEAD_DIM // 2, A_HEADS)
    ka = _rope(_head_norm(ka, A_KV_HEADS, A_HEAD_DIM) * gk_ref[...], ca, sa1, sa2, A_HEAD_DIM // 2,
               A_KV_HEADS)

    qb = _rope(_dot(_rms(cq, mq_ref[...]), wuq_ref[...]), cb, sb1, sb2, B_ROPE // 2, B_HEADS)
    kvb = _dot(_rms(ckv, mkv_ref[...]), wukv_ref[...])
    kr = _rope(kr, cb, sb1, sb2, B_ROPE // 2, 1)
    kb = kvb[:, :B_HEADS * LANE] + jnp.tile(kr, (1, B_HEADS))
    vb = kvb[:, B_HEADS * LANE:]

    q_ref[...] = jnp.concatenate([qa, qb], axis=1).astype(BF16)
    per_query_head = lambda x: [x[:, g * LANE:(g + 1) * LANE] for g in range(A_KV_HEADS)
                                for _ in range(A_GROUP)]
    k_ref[...] = jnp.concatenate(per_query_head(ka) + [kb], axis=1).astype(BF16)
    v_ref[...] = jnp.concatenate(per_query_head(va) + [vb], axis=1).astype(BF16)


def _attn_proj(x, lw, seq, tile):
    n = x.shape[0]
    nt = n // tile
    per_seq = seq // tile
    full = lambda a: pl.BlockSpec(a.shape, lambda i: (0,) * a.ndim)
    tok = lambda w: pl.BlockSpec((tile, w), lambda i: (i, 0))
    tab = pl.BlockSpec((tile, LANE), lambda i: (i % per_seq, 0))
    weights = (lw["g"], lw["w_all"], lw["gq"], lw["gk"], lw["mq"], lw["wuq"], lw["mkv"], lw["wukv"])
    return pl.pallas_call(
        _attn_proj_kernel,
        grid=(nt,),
        in_specs=[tok(D_MODEL)] + [full(a) for a in weights] + [tab] * 6,
        out_specs=[tok(N_HEADS * LANE)] * 3,
        out_shape=[jax.ShapeDtypeStruct((n, N_HEADS * LANE), BF16)] * 3,
        compiler_params=_params("parallel"),
        name="attn_proj",
    )(x, *weights, *lw["rope"])


def _attn_kernel(q_ref, k_ref, v_ref, o_ref):
    pair = pl.program_id(1)
    scale = jnp.where(pair < A_HEADS // HEADS_PER_STEP, A_HEAD_DIM ** -0.5, B_QK ** -0.5).astype(F32)
    c = scale * LOG2_E
    heads = [slice(h * LANE, (h + 1) * LANE) for h in range(HEADS_PER_STEP)]
    scores = [_dot_nt(q_ref[:, sl], k_ref[:, sl]) for sl in heads]
    outs = []
    for s, sl in zip(scores, heads):
        m = jnp.max(s, axis=-1, keepdims=True)
        p = jnp.exp2((s - m) * c)
        l = jnp.sum(p, axis=-1, keepdims=True)
        outs.append((_dot(p, v_ref[:, sl]) / l)[:, :A_HEAD_DIM])
    o_ref[...] = jnp.concatenate(outs, axis=1).astype(BF16)


def _attention(q, k, v, batch, seq, tq):
    n = q.shape[0]
    nq = seq // tq
    width = HEADS_PER_STEP * LANE
    return pl.pallas_call(
        _attn_kernel,
        grid=(batch, N_HEADS // HEADS_PER_STEP, nq),
        in_specs=[pl.BlockSpec((tq, width), lambda b, h, i: (b * nq + i, h)),
                  pl.BlockSpec((seq, width), lambda b, h, i: (b, h)),
                  pl.BlockSpec((seq, width), lambda b, h, i: (b, h))],
        out_specs=pl.BlockSpec((tq, HEADS_PER_STEP * A_HEAD_DIM), lambda b, h, i: (b * nq + i, h)),
        out_shape=jax.ShapeDtypeStruct((n, N_HEADS * A_HEAD_DIM), BF16),
        compiler_params=_params("parallel", "parallel", "parallel"),
        name="attention",
    )(q, k, v)


def _attn_out_kernel(o_ref, x_ref, w_ref, g_ref, wrt_ref, x1_ref, xn_ref, aff_ref):
    x1 = x_ref[...] + _dot(o_ref[...], w_ref[...])
    x1_ref[...] = x1
    _router_tail(x1, g_ref, wrt_ref, xn_ref, aff_ref)


def _tail_specs(n, tile):
    out_specs = [pl.BlockSpec((tile, D_MODEL), lambda i: (i, 0)),
                 pl.BlockSpec((tile, D_MODEL), lambda i: (i, 0)),
                 pl.BlockSpec((N_EXPERTS, tile), lambda i: (0, i))]
    out_shape = [jax.ShapeDtypeStruct((n, D_MODEL), F32),
                 jax.ShapeDtypeStruct((n, D_MODEL), F32),
                 jax.ShapeDtypeStruct((N_EXPERTS, n), F32)]
    return out_specs, out_shape


def _attn_out(o, x, lw, fw, tile):
    n = x.shape[0]
    full = lambda a: pl.BlockSpec(a.shape, lambda i: (0,) * a.ndim)
    out_specs, out_shape = _tail_specs(n, tile)
    weights = (lw["w_out"], fw["g"], fw["wrt"])
    return pl.pallas_call(
        _attn_out_kernel,
        grid=(n // tile,),
        in_specs=[pl.BlockSpec((tile, N_HEADS * A_HEAD_DIM), lambda i: (i, 0)),
                  pl.BlockSpec((tile, D_MODEL), lambda i: (i, 0))] + [full(a) for a in weights],
        out_specs=out_specs,
        out_shape=out_shape,
        compiler_params=_params("parallel"),
        name="attn_out",
    )(o, x, *weights)


def _sgu_kernel(x_ref, g_ref, win_ref, lng_ref, lnb_ref, ws_ref, bs_ref, wout_ref, gf_ref, wrt_ref,
                x1_ref, xn_ref, aff_ref):
    x = x_ref[...]
    tile = x.shape[0]
    z = jax.nn.gelu(_dot(_rms(x, g_ref[...]), win_ref[...]))
    u, v = z[:, :C_WIDTH], z[:, C_WIDTH:]
    mu = jnp.mean(v, axis=-1, keepdims=True)
    vc = v - mu
    v = vc * lax.rsqrt(jnp.mean(vc * vc, axis=-1, keepdims=True) + EPS) * lng_ref[...] + lnb_ref[...]
    v = v.astype(BF16)
    bias = bs_ref[...]
    rows = []
    for c in range(tile // C_CHUNK):
        vch = v[c * C_CHUNK:(c + 1) * C_CHUNK, :]
        cols = [_dot(ws_ref[g], vch[:, g * C_GROUP_W:(g + 1) * C_GROUP_W]) for g in range(C_GROUPS)]
        rows.append(jnp.concatenate(cols, axis=1) + bias)
    s = jnp.concatenate(rows, axis=0)
    x1 = x + _dot(u * s, wout_ref[...])
    x1_ref[...] = x1
    _router_tail(x1, gf_ref, wrt_ref, xn_ref, aff_ref)


def _sgu(x, lw, fw, tile):
    n = x.shape[0]
    full = lambda a: pl.BlockSpec(a.shape, lambda i: (0,) * a.ndim)
    out_specs, out_shape = _tail_specs(n, tile)
    weights = (lw["g"], lw["w_in"], lw["ln_g"], lw["ln_b"], lw["w_s"], lw["b_s"], lw["w_out"],
               fw["g"], fw["wrt"])
    return pl.pallas_call(
        _sgu_kernel,
        grid=(n // tile,),
        in_specs=[pl.BlockSpec((tile, D_MODEL), lambda i: (i, 0))] + [full(a) for a in weights],
        out_specs=out_specs,
        out_shape=out_shape,
        compiler_params=_params("parallel"),
        name="sgu",
    )(x, *weights)


def _threshold_kernel(aff_ref, thr_ref, need_ref, *, cap):
    def count(pred):
        return jnp.sum(jnp.where(pred, 1.0, 0.0), axis=1, keepdims=True)

    def body(i, cur):
        cand = cur | jnp.left_shift(jnp.int32(1), 30 - i)
        bits = lax.bitcast_convert_type(aff_ref[...], jnp.int32)
        return jnp.where(count(bits >= cand) >= cap, cand, cur)

    thr = lax.fori_loop(0, 31, body, jnp.zeros((N_EXPERTS, 1), jnp.int32))
    bits = lax.bitcast_convert_type(aff_ref[...], jnp.int32)
    thr_ref[...] = thr
    need_ref[...] = cap - count(bits > thr)


def _threshold(aff_t, cap):
    n = aff_t.shape[1]
    small = lambda dt: (pl.BlockSpec((N_EXPERTS, 1), lambda i: (0, 0)),
                        jax.ShapeDtypeStruct((N_EXPERTS, 1), dt))
    (s0, o0), (s1, o1) = small(jnp.int32), small(F32)
    return pl.pallas_call(
        functools.partial(_threshold_kernel, cap=cap),
        grid=(1,),
        in_specs=[pl.BlockSpec((N_EXPERTS, n), lambda i: (0, 0))],
        out_specs=[s0, s1],
        out_shape=[o0, o1],
        compiler_params=_params("arbitrary"),
        name="ec_threshold",
    )(aff_t)


def _compact_kernel(aff_ref, thr_ref, need_ref, slot_ref, offs_ref, carry_ref, tie_ref, *, chunks, step_tokens):
    step = pl.program_id(0)

    @pl.when(step == 0)
    def _():
        slot_ref[...] = jnp.zeros_like(slot_ref)
        carry_ref[...] = jnp.zeros_like(carry_ref)
        tie_ref[...] = jnp.zeros_like(tie_ref)

    offs_ref[0] = jnp.broadcast_to(carry_ref[...], (N_EXPERTS, LANE))

    row_i = lax.broadcasted_iota(jnp.int32, (SEL_BLOCK, SEL_BLOCK), 0)
    col_i = lax.broadcasted_iota(jnp.int32, (SEL_BLOCK, SEL_BLOCK), 1)
    upper = jnp.where(row_i < col_i, 1.0, 0.0).astype(BF16)
    lane = lax.broadcasted_iota(jnp.int32, (N_EXPERTS, SEL_BLOCK), 1)
    shifts = [1 << b for b in range(SEL_BLOCK.bit_length() - 1)]
    thr = thr_ref[...]
    need = need_ref[...]

    for sb in range(step_tokens // SEL_BLOCK):
        aff = aff_ref[:, sb * SEL_BLOCK:(sb + 1) * SEL_BLOCK]
        bits = lax.bitcast_convert_type(aff, jnp.int32)
        eq = bits == thr
        eq_f = jnp.where(eq, 1.0, 0.0)
        ties_before = jnp.dot(eq_f.astype(BF16), upper, preferred_element_type=F32) + tie_ref[...]
        sel = (bits > thr) | (eq & (ties_before < need))
        sel_f = jnp.where(sel, 1.0, 0.0)
        tie_ref[...] = tie_ref[...] + jnp.sum(eq_f, axis=1, keepdims=True)

        before = jnp.dot(sel_f.astype(BF16), upper, preferred_element_type=F32)
        count = jnp.sum(sel_f, axis=1, keepdims=True).astype(jnp.int32)
        carry = carry_ref[...]
        tok = (step * step_tokens + sb * SEL_BLOCK + lane).astype(F32)

        dist = jnp.where(sel, lane - before.astype(jnp.int32), 0)
        ids = jnp.where(sel, tok, 0.0)
        gates = jnp.where(sel, aff, 0.0)
        for sh in shifts:
            move = (dist & sh) != 0
            step_left = lambda x, zero: (jnp.where(move, zero, x)
                                         + pltpu.roll(jnp.where(move, x, zero), SEL_BLOCK - sh, 1))
            ids, gates, dist = step_left(ids, 0.0), step_left(gates, 0.0), step_left(dist, 0)

        off = carry & (SEL_BLOCK - 1)
        for sh in shifts:
            turn = (off & sh) != 0
            ids = jnp.where(turn, pltpu.roll(ids, sh, 1), ids)
            gates = jnp.where(turn, pltpu.roll(gates, sh, 1), gates)
        end = off + count
        in_lo = (lane >= off) & (lane < end)
        in_hi = lane < end - SEL_BLOCK
        pick = lambda m, x, e: jnp.where(m, x, 0.0)[e:e + 1, :]

        for e in range(N_EXPERTS):
            row = e * chunks + carry[e, 0] // SEL_BLOCK
            lo = jnp.concatenate([pick(in_lo, ids, e), pick(in_lo, gates, e)], axis=0)
            hi = jnp.concatenate([pick(in_hi, ids, e), pick(in_hi, gates, e)], axis=0)
            slot_ref[row] = slot_ref[row] + lo
            slot_ref[row + 1] = slot_ref[row + 1] + hi

        carry_ref[...] = carry + count


def _compact(aff_t, thr, need, chunks, step_tokens):
    n = aff_t.shape[1]
    n_steps = n // step_tokens
    n_rows = N_EXPERTS * chunks + 1
    small = lambda a: pl.BlockSpec(a.shape, lambda i: (0, 0))
    return pl.pallas_call(
        functools.partial(_compact_kernel, chunks=chunks, step_tokens=step_tokens),
        grid=(n_steps,),
        in_specs=[pl.BlockSpec((N_EXPERTS, step_tokens), lambda i: (0, i)), small(thr), small(need)],
        out_specs=[pl.BlockSpec((n_rows, 2, SEL_BLOCK), lambda i: (0, 0, 0)),
                   pl.BlockSpec((1, N_EXPERTS, LANE), lambda i: (i, 0, 0))],
        out_shape=[jax.ShapeDtypeStruct((n_rows, 2, SEL_BLOCK), F32),
                   jax.ShapeDtypeStruct((n_steps, N_EXPERTS, LANE), jnp.int32)],
        scratch_shapes=[pltpu.VMEM((N_EXPERTS, 1), jnp.int32), pltpu.VMEM((N_EXPERTS, 1), F32)],
        compiler_params=_params("arbitrary"),
        name="ec_compact",
    )(aff_t, thr, need)


def _ffn_kernel(idx_ref, gate_ref, xn_hbm, wg_ref, wu_ref, wd_ref, y_ref, buf0, buf1, sem, *, rows, n_pairs):
    pair = pl.program_id(0) * pl.num_programs(1) + pl.program_id(1)

    def row_copy(tok, buf, r, s):
        return pltpu.make_async_copy(xn_hbm.at[pl.ds(tok, 1), :], buf.at[pl.ds(r, 1), :], sem.at[s])

    def gather(chunk, buf, s):
        for r in range(rows):
            row_copy(idx_ref[chunk * rows + r], buf, r, s).start()

    def wait(buf, s):
        pltpu.make_async_copy(xn_hbm.at[pl.ds(0, rows), :], buf, sem.at[s]).wait()

    def compute(buf, half):
        xg = buf[...].astype(BF16)
        hid = jax.nn.silu(_dot(xg, wg_ref[0, 0])) * _dot(xg, wu_ref[0, 0])
        y = _dot(hid, wd_ref[0, 0]) * gate_ref[half * rows:(half + 1) * rows, :]
        y_ref[half * rows:(half + 1) * rows] = y.reshape(rows, 1, D_MODEL)

    @pl.when(pair == 0)
    def _():
        def body(r, c):
            row_copy(idx_ref[r], buf0, r, 0).start()
            return c
        lax.fori_loop(0, rows, body, 0)

    wait(buf0, 0)
    gather(2 * pair + 1, buf1, 1)
    compute(buf0, 0)
    wait(buf1, 1)
    gather(jnp.where(pair + 1 < n_pairs, 2 * pair + 2, 0), buf0, 0)
    compute(buf1, 1)

    @pl.when(pair == n_pairs - 1)
    def _():
        wait(buf0, 0)


def _ffn(xn, idx, gate, fw, cap, rows):
    pairs = cap // (2 * rows)
    layer = fw["layer"]
    w_spec = lambda shape: pl.BlockSpec((1, 1) + shape, lambda e, c, idx: (layer, e, 0, 0))
    grid_spec = pltpu.PrefetchScalarGridSpec(
        num_scalar_prefetch=1,
        grid=(N_EXPERTS, pairs),
        in_specs=[pl.BlockSpec((2 * rows, 1), lambda e, c, idx: (e * pairs + c, 0)),
                  pl.BlockSpec(memory_space=pl.ANY),
                  w_spec((D_MODEL, EXPERT_FF)), w_spec((D_MODEL, EXPERT_FF)), w_spec((EXPERT_FF, D_MODEL))],
        out_specs=pl.BlockSpec((2 * rows, 1, D_MODEL), lambda e, c, idx: (e * pairs + c, 0, 0)),
        scratch_shapes=[pltpu.VMEM((rows, D_MODEL), F32), pltpu.VMEM((rows, D_MODEL), F32),
                        pltpu.SemaphoreType.DMA((2,))],
    )
    return pl.pallas_call(
        functools.partial(_ffn_kernel, rows=rows, n_pairs=N_EXPERTS * pairs),
        grid_spec=grid_spec,
        out_shape=jax.ShapeDtypeStruct((N_EXPERTS * cap, 1, D_MODEL), F32),
        compiler_params=pltpu.CompilerParams(dimension_semantics=("arbitrary", "arbitrary"),
                                             vmem_limit_bytes=VMEM_LIMIT, disable_bounds_checks=True),
        name="ec_ffn",
    )(idx, gate, xn, fw["w_gate"], fw["w_up"], fw["w_down"])


def _combine_kernel(offs_ref, idx_ref, x1_ref, y_hbm, *rest, cap, n_tiles, tile, width, normalize):
    norm_g, (o_ref, acc, ywin, yover, sem, osem) = (rest[0], rest[1:]) if normalize else (None, rest)
    j = pl.program_id(0)
    slot = j % 2
    total = N_EXPERTS * cap

    def window(t, e):
        first = e * cap + offs_ref[t * N_EXPERTS + e]
        return first, jnp.minimum(first, total - width)

    def win_copy(t, e, s):
        return pltpu.make_async_copy(y_hbm.at[pl.ds(window(t, e)[1], width)], ywin.at[s, e], sem.at[s])

    def fetch(t, s):
        for e in range(N_EXPERTS):
            win_copy(t, e, s).start()

    def add_rows(win, first, shift, count):
        def group(g, partial):
            work = []
            for u in range(ROW_UNROLL):
                r = g * ROW_UNROLL + u
                if partial:
                    tok = idx_ref[jnp.minimum(first + r, total - 1)] - j * tile
                    dst = jnp.where(r < count, tok, tile + u)
                    src = jnp.minimum(shift + r, width - 1)
                else:
                    dst = idx_ref[first + r] - j * tile
                    src = shift + r
                work.append((dst, acc[dst] + win[src]))
            for dst, val in work:
                acc[dst] = val

        def body(g, c):
            group(g, False)
            return c
        whole = count // ROW_UNROLL
        lax.fori_loop(0, whole, body, 0)
        group(whole, True)

    @pl.when(j == 0)
    def _():
        fetch(0, 0)

    @pl.when(j + 1 < n_tiles)
    def _():
        fetch(j + 1, 1 - slot)

    acc[...] = jnp.zeros_like(acc)
    for e in range(N_EXPERTS):
        win_copy(j, e, slot).wait()

    for e in range(N_EXPERTS):
        first, start = window(j, e)
        count = offs_ref[(j + 1) * N_EXPERTS + e] - offs_ref[j * N_EXPERTS + e]
        add_rows(ywin.at[slot, e], first, first - start, jnp.minimum(count, width))

        def overflow(k, c, first=first, count=count):
            begin = first + k * width
            start = jnp.minimum(begin, total - width)
            cp = pltpu.make_async_copy(y_hbm.at[pl.ds(start, width)], yover, osem)
            cp.start()
            cp.wait()
            add_rows(yover, begin, begin - start, jnp.minimum(count - k * width, width))
            return c
        lax.fori_loop(1, (count + width - 1) // width, overflow, 0)

    out = x1_ref[...] + acc[0:tile].reshape(tile, D_MODEL)
    o_ref[...] = _rms(out, norm_g[...]) if normalize else out


def _combine(x1, y, idx, offs, cap, tile, width, final_g):
    extra = () if final_g is None else (final_g,)
    n = x1.shape[0]
    n_tiles = n // tile
    grid_spec = pltpu.PrefetchScalarGridSpec(
        num_scalar_prefetch=2,
        grid=(n_tiles,),
        in_specs=[pl.BlockSpec((tile, D_MODEL), lambda i, offs, idx: (i, 0)),
                  pl.BlockSpec(memory_space=pl.ANY)]
                 + [pl.BlockSpec((1, D_MODEL), lambda i, offs, idx: (0, 0)) for _ in extra],
        out_specs=pl.BlockSpec((tile, D_MODEL), lambda i, offs, idx: (i, 0)),
        scratch_shapes=[pltpu.VMEM((tile + ROW_UNROLL, 1, D_MODEL), F32),
                        pltpu.VMEM((2, N_EXPERTS, width, 1, D_MODEL), F32),
                        pltpu.VMEM((width, 1, D_MODEL), F32),
                        pltpu.SemaphoreType.DMA((2,)), pltpu.SemaphoreType.DMA(())],
    )
    return pl.pallas_call(
        functools.partial(_combine_kernel, cap=cap, n_tiles=n_tiles, tile=tile, width=width,
                          normalize=final_g is not None),
        grid_spec=grid_spec,
        out_shape=jax.ShapeDtypeStruct((n, D_MODEL), F32),
        compiler_params=_params("arbitrary"),
        name="ec_combine",
    )(offs, idx, x1, y, *extra)


def _ec_ffn(x1, xn, aff_t, fw, final_g):
    n = x1.shape[0]
    cap = EC_CAPACITY_FACTOR * n // N_EXPERTS
    chunks = cap // SEL_BLOCK
    step_tokens = min(SEL_STEP, n)
    thr, need = _threshold(aff_t, cap)
    slots, offs_t = _compact(aff_t, thr, need, chunks, step_tokens)
    idx = slots[:N_EXPERTS * chunks, 0, :].reshape(-1).astype(jnp.int32)
    gate = slots[:N_EXPERTS * chunks, 1, :].reshape(-1, 1)
    offs = jnp.concatenate([offs_t[:, :, 0], jnp.full((1, N_EXPERTS), cap, jnp.int32)], axis=0).reshape(-1)
    y = _ffn(xn, idx, gate, fw, cap, min(FFN_ROWS, cap // 2))
    return _combine(x1, y, idx, offs, cap, step_tokens, min(COMBINE_W, cap), final_g)


def _pad_heads(w, n_heads, width):
    lead = w.shape[:-1]
    w = w.reshape(lead + (n_heads, width))
    w = jnp.pad(w, [(0, 0)] * len(lead) + [(0, 0), (0, LANE - width)])
    return w.reshape(lead + (n_heads * LANE,))


def _rope_tables(seq):
    rows = seq // GRID_W
    row = jnp.repeat(jnp.arange(rows, dtype=F32), GRID_W)
    col = jnp.tile(jnp.arange(GRID_W, dtype=F32), rows)

    def cos_sin(rot_dim):
        n_pair = rot_dim // 4
        freq = ROPE_THETA ** (-jnp.arange(n_pair, dtype=F32) / n_pair)
        ang = jnp.concatenate([row[:, None] * freq, col[:, None] * freq], axis=-1)
        return jnp.cos(ang), jnp.sin(ang)

    def place(parts):
        out = jnp.zeros((seq, LANE), F32)
        for start, val in parts:
            out = out.at[:, start:start + val.shape[1]].set(val)
        return out

    ca, sa = cos_sin(A_HEAD_DIM)
    cb, sb = cos_sin(B_ROPE)
    ha, hb = A_HEAD_DIM // 2, B_ROPE // 2
    ones = jnp.ones((seq, B_NOPE), F32)
    return (place([(0, ca), (ha, ca)]), place([(0, -sa)]), place([(ha, sa)]),
            place([(0, ones), (B_NOPE, cb), (B_NOPE + hb, cb)]), place([(B_NOPE, -sb)]),
            place([(B_NOPE + hb, sb)]))


def _prep_attn_layer(i, p, rope):
    w_in = p["w_in_attn"][i]
    o = 0
    w_qa = w_in[:, o:o + A_Q_W]; o += A_Q_W
    w_ka = w_in[:, o:o + A_KV_W]; o += A_KV_W
    w_va = w_in[:, o:o + A_KV_W]; o += A_KV_W
    w_cq = w_in[:, o:o + B_Q_RANK]; o += B_Q_RANK
    w_ckv = w_in[:, o:o + B_KV_RANK]; o += B_KV_RANK
    w_kr = w_in[:, o:o + B_ROPE]
    w_kr = jnp.pad(w_kr, [(0, 0), (B_NOPE, LANE - B_NOPE - B_ROPE)])
    w_all = jnp.concatenate([_pad_heads(w_qa, A_HEADS, A_HEAD_DIM), _pad_heads(w_ka, A_KV_HEADS, A_HEAD_DIM),
                             _pad_heads(w_va, A_KV_HEADS, A_HEAD_DIM), w_cq, w_ckv, w_kr], axis=1)
    w_ukv = p["w_ukv"][i].reshape(B_KV_RANK, B_HEADS, B_NOPE + B_V)
    w_uk = _pad_heads(w_ukv[:, :, :B_NOPE].reshape(B_KV_RANK, -1), B_HEADS, B_NOPE)
    w_uv = _pad_heads(w_ukv[:, :, B_NOPE:].reshape(B_KV_RANK, -1), B_HEADS, B_V)
    return {
        "g": p["attn_norm"][i][None, :],
        "w_all": w_all.astype(BF16),
        "gq": jnp.tile(jnp.pad(p["qk_norm_q"][i], (0, LANE - A_HEAD_DIM)), A_HEADS)[None, :],
        "gk": jnp.tile(jnp.pad(p["qk_norm_k"][i], (0, LANE - A_HEAD_DIM)), A_KV_HEADS)[None, :],
        "mq": p["mla_q_norm"][i][None, :],
        "wuq": _pad_heads(p["w_uq"][i], B_HEADS, B_QK).astype(BF16),
        "mkv": p["mla_kv_norm"][i][None, :],
        "wukv": jnp.concatenate([w_uk, w_uv], axis=1).astype(BF16),
        "w_out": p["w_out_attn"][i].astype(BF16),
        "rope": rope,
    }


def _prep_sgu_layer(i, p):
    b_s = p["b_spatial"][i]
    return {
        "g": p["sgu_norm"][i][None, :],
        "w_in": p["w_in_sgu"][i].astype(BF16),
        "ln_g": p["sgu_ln_g"][i][None, :],
        "ln_b": p["sgu_ln_b"][i][None, :],
        "w_s": p["w_spatial"][i].astype(BF16),
        "b_s": jnp.repeat(b_s.T, C_GROUP_W, axis=1),
        "w_out": p["w_out_sgu"][i].astype(BF16),
    }


def _prep_ffn_layer(l, p, w_gate, w_up, w_down):
    return {
        "g": p["ffn_norm"][l][None, :],
        "wrt": p["w_router"][l].T.astype(BF16),
        "layer": l, "w_gate": w_gate, "w_up": w_up, "w_down": w_down,
    }


def _pick(n, target):
    t = min(n, target)
    while n % t:
        t //= 2
    return t


def _trunk(x, attn_layers, sgu_layers, ffn_layers, final_g):
    batch, seq, _ = x.shape
    n = batch * seq
    x = x.reshape(n, D_MODEL)
    tile = _pick(seq, 512)
    tq = _pick(seq, 256)
    for l, fw in enumerate(ffn_layers):
        if l % 2 == 0:
            lw = attn_layers[l // 2]
            q, k, v = _attn_proj(x, lw, seq, tile)
            o = _attention(q, k, v, batch, seq, tq)
            x1, xn, aff_t = _attn_out(o, x, lw, fw, tile)
        else:
            x1, xn, aff_t = _sgu(x, sgu_layers[l // 2], fw, tile)
        last = l == len(ffn_layers) - 1
        x = _ec_ffn(x1, xn, aff_t, fw, final_g[None, :] if last else None)
    return x.reshape(batch, seq, D_MODEL)


def kernel(x_prompt, x_sample, attn_norm, w_in_attn, qk_norm_q, qk_norm_k, mla_q_norm, w_uq, mla_kv_norm, w_ukv, w_out_attn, sgu_norm, w_in_sgu, sgu_ln_g, sgu_ln_b, w_spatial, b_spatial, w_out_sgu, ffn_norm, w_router, w_gate, w_up, w_down, final_norm):
    p = dict(attn_norm=attn_norm, w_in_attn=w_in_attn, qk_norm_q=qk_norm_q, qk_norm_k=qk_norm_k,
             mla_q_norm=mla_q_norm, w_uq=w_uq, mla_kv_norm=mla_kv_norm, w_ukv=w_ukv,
             w_out_attn=w_out_attn, sgu_norm=sgu_norm, w_in_sgu=w_in_sgu, sgu_ln_g=sgu_ln_g,
             sgu_ln_b=sgu_ln_b, w_spatial=w_spatial, b_spatial=b_spatial, w_out_sgu=w_out_sgu,
             ffn_norm=ffn_norm, w_router=w_router)
    depth = ffn_norm.shape[0]
    wg, wu, wd = w_gate.astype(BF16), w_up.astype(BF16), w_down.astype(BF16)
    ffn_layers = [_prep_ffn_layer(l, p, wg, wu, wd) for l in range(depth)]
    sgu_layers = [_prep_sgu_layer(i, p) for i in range(sgu_norm.shape[0])]
    outs = []
    for x in (x_prompt, x_sample):
        rope = _rope_tables(x.shape[1])
        attn_layers = [_prep_attn_layer(i, p, rope) for i in range(attn_norm.shape[0])]
        outs.append(_trunk(x, attn_layers, sgu_layers, ffn_layers, final_norm))
    return tuple(outs)
```

```python
import functools

import jax
import jax.numpy as jnp
from jax import lax
from jax.experimental import pallas as pl
from jax.experimental.pallas import tpu as pltpu

D_MODEL = 1024
GRID_W = 64
ROPE_THETA = 10000.0
EPS = 1e-6

A_HEADS = 8
A_KV_HEADS = 2
A_GROUP = A_HEADS // A_KV_HEADS
A_HEAD_DIM = 64
A_Q_W = A_HEADS * A_HEAD_DIM
A_KV_W = A_KV_HEADS * A_HEAD_DIM

B_HEADS = 8
B_Q_RANK = 384
B_KV_RANK = 256
B_NOPE = 64
B_ROPE = 32
B_V = 64
B_QK = B_NOPE + B_ROPE

N_HEADS = A_HEADS + B_HEADS
HEADS_PER_STEP = 4
LOG2_E = 1.4426950408889634

C_WIDTH = D_MODEL
C_GROUPS = 8
C_GROUP_W = C_WIDTH // C_GROUPS
C_CHUNK = 128

N_EXPERTS = 16
EXPERT_FF = 2 * D_MODEL
EC_CAPACITY_FACTOR = 2

LANE = 128
VMEM_LIMIT = 56 * 1024 * 1024
SEL_BLOCK = LANE
SEL_STEP = 1024
FFN_ROWS = 512
COMBINE_W = 176
ROW_UNROLL = 8
BF16 = jnp.bfloat16
F32 = jnp.float32


def _params(*sem):
    return pltpu.CompilerParams(dimension_semantics=sem, vmem_limit_bytes=VMEM_LIMIT)


def _dot(a, b):
    return jnp.dot(a.astype(BF16), b.astype(BF16), preferred_element_type=F32)


def _dot_nt(a, b):
    return lax.dot_general(a.astype(BF16), b.astype(BF16), (((1,), (1,)), ((), ())),
                           preferred_element_type=F32)


def _rms(x, g):
    return x * lax.rsqrt(jnp.mean(x * x, axis=-1, keepdims=True) + EPS) * g


def _router_tail(x1, g_ref, wrt_ref, xn_ref, aff_ref):
    xn = _rms(x1, g_ref[...])
    xn_ref[...] = xn
    logits = _dot_nt(wrt_ref[...], xn)
    m = jnp.max(logits, axis=0, keepdims=True)
    e = jnp.exp(logits - m)
    aff_ref[...] = e / jnp.sum(e, axis=0, keepdims=True)


def _head_norm(z, n_heads, width):
    outs = []
    for h in range(n_heads):
        blk = z[:, h * LANE:(h + 1) * LANE]
        ms = jnp.sum(blk * blk, axis=-1, keepdims=True) * (1.0 / width)
        outs.append(blk * lax.rsqrt(ms + EPS))
    return jnp.concatenate(outs, axis=1)


def _rope(x, c, s_lo, s_hi, half, n_heads):
    w = x.shape[1]
    c = jnp.tile(c, (1, n_heads))
    s_lo = jnp.tile(s_lo, (1, n_heads))
    s_hi = jnp.tile(s_hi, (1, n_heads))
    return x * c + pltpu.roll(x, w - half, 1) * s_lo + pltpu.roll(x, half, 1) * s_hi


def _attn_proj_kernel(x_ref, g_ref, w_ref, gq_ref, gk_ref, mq_ref, wuq_ref, mkv_ref, wukv_ref,
                      ca_ref, sa1_ref, sa2_ref, cb_ref, sb1_ref, sb2_ref,
                      q_ref, k_ref, v_ref):
    h = _rms(x_ref[...], g_ref[...])
    z = _dot(h, w_ref[...])
    o = 0
    qa = z[:, o:o + A_HEADS * LANE]; o += A_HEADS * LANE
    ka = z[:, o:o + A_KV_HEADS * LANE]; o += A_KV_HEADS * LANE
    va = z[:, o:o + A_KV_HEADS * LANE]; o += A_KV_HEADS * LANE
    cq = z[:, o:o + B_Q_RANK]; o += B_Q_RANK
    ckv = z[:, o:o + B_KV_RANK]; o += B_KV_RANK
    kr = z[:, o:o + LANE]

    ca, sa1, sa2 = ca_ref[...], sa1_ref[...], sa2_ref[...]
    cb, sb1, sb2 = cb_ref[...], sb1_ref[...], sb2_ref[...]

    qa = _rope(_head_norm(qa, A_HEADS, A_HEAD_DIM) * gq_ref[...], ca, sa1, sa2, A_HEAD_DIM // 2, A_HEADS)
    ka = _rope(_head_norm(ka, A_KV_HEADS, A_HEAD_DIM) * gk_ref[...], ca, sa1, sa2, A_HEAD_DIM // 2,
               A_KV_HEADS)

    qb = _rope(_dot(_rms(cq, mq_ref[...]), wuq_ref[...]), cb, sb1, sb2, B_ROPE // 2, B_HEADS)
    kvb = _dot(_rms(ckv, mkv_ref[...]), wukv_ref[...])
    kr = _rope(kr, cb, sb1, sb2, B_ROPE // 2, 1)
    kb = kvb[:, :B_HEADS * LANE] + jnp.tile(kr, (1, B_HEADS))
    vb = kvb[:, B_HEADS * LANE:]

    q_ref[...] = jnp.concatenate([qa, qb], axis=1).astype(BF16)
    per_query_head = lambda x: [x[:, g * LANE:(g + 1) * LANE] for g in range(A_KV_HEADS)
                                for _ in range(A_GROUP)]
    k_ref[...] = jnp.concatenate(per_query_head(ka) + [kb], axis=1).astype(BF16)
    v_ref[...] = jnp.concatenate(per_query_head(va) + [vb], axis=1).astype(BF16)


def _attn_proj(x, lw, seq, tile):
    n = x.shape[0]
    nt = n // tile
    per_seq = seq // tile
    full = lambda a: pl.BlockSpec(a.shape, lambda i: (0,) * a.ndim)
    tok = lambda w: pl.BlockSpec((tile, w), lambda i: (i, 0))
    tab = pl.BlockSpec((tile, LANE), lambda i: (i % per_seq, 0))
    weights = (lw["g"], lw["w_all"], lw["gq"], lw["gk"], lw["mq"], lw["wuq"], lw["mkv"], lw["wukv"])
    return pl.pallas_call(
        _attn_proj_kernel,
        grid=(nt,),
        in_specs=[tok(D_MODEL)] + [full(a) for a in weights] + [tab] * 6,
        out_specs=[tok(N_HEADS * LANE)] * 3,
        out_shape=[jax.ShapeDtypeStruct((n, N_HEADS * LANE), BF16)] * 3,
        compiler_params=_params("parallel"),
        name="attn_proj",
    )(x, *weights, *lw["rope"])


def _attn_kernel(q_ref, k_ref, v_ref, o_ref):
    pair = pl.program_id(1)
    scale = jnp.where(pair < A_HEADS // HEADS_PER_STEP, A_HEAD_DIM ** -0.5, B_QK ** -0.5).astype(F32)
    c = scale * LOG2_E
    heads = [slice(h * LANE, (h + 1) * LANE) for h in range(HEADS_PER_STEP)]
    scores = [_dot_nt(q_ref[:, sl], k_ref[:, sl]) for sl in heads]
    outs = []
    for s, sl in zip(scores, heads):
        m = jnp.max(s, axis=-1, keepdims=True)
        p = jnp.exp2((s - m) * c)
        l = jnp.sum(p, axis=-1, keepdims=True)
        outs.append((_dot(p, v_ref[:, sl]) / l)[:, :A_HEAD_DIM])
    o_ref[...] = jnp.concatenate(outs, axis=1).astype(BF16)


def _attention(q, k, v, batch, seq, tq):
    n = q.shape[0]
    nq = seq // tq
    width = HEADS_PER_STEP * LANE
    return pl.pallas_call(
        _attn_kernel,
        grid=(batch, N_HEADS // HEADS_PER_STEP, nq),
        in_specs=[pl.BlockSpec((tq, width), lambda b, h, i: (b * nq + i, h)),
                  pl.BlockSpec((seq, width), lambda b, h, i: (b, h)),
                  pl.BlockSpec((seq, width), lambda b, h, i: (b, h))],
        out_specs=pl.BlockSpec((tq, HEADS_PER_STEP * A_HEAD_DIM), lambda b, h, i: (b * nq + i, h)),
        out_shape=jax.ShapeDtypeStruct((n, N_HEADS * A_HEAD_DIM), BF16),
        compiler_params=_params("parallel", "parallel", "parallel"),
        name="attention",
    )(q, k, v)


def _attn_out_kernel(o_ref, x_ref, w_ref, g_ref, wrt_ref, x1_ref, xn_ref, aff_ref):
    x1 = x_ref[...] + _dot(o_ref[...], w_ref[...])
    x1_ref[...] = x1
    _router_tail(x1, g_ref, wrt_ref, xn_ref, aff_ref)


def _tail_specs(n, tile):
    out_specs = [pl.BlockSpec((tile, D_MODEL), lambda i: (i, 0)),
                 pl.BlockSpec((tile, D_MODEL), lambda i: (i, 0)),
                 pl.BlockSpec((N_EXPERTS, tile), lambda i: (0, i))]
    out_shape = [jax.ShapeDtypeStruct((n, D_MODEL), F32),
                 jax.ShapeDtypeStruct((n, D_MODEL), F32),
                 jax.ShapeDtypeStruct((N_EXPERTS, n), F32)]
    return out_specs, out_shape


def _attn_out(o, x, lw, fw, tile):
    n = x.shape[0]
    full = lambda a: pl.BlockSpec(a.shape, lambda i: (0,) * a.ndim)
    out_specs, out_shape = _tail_specs(n, tile)
    weights = (lw["w_out"], fw["g"], fw["wrt"])
    return pl.pallas_call(
        _attn_out_kernel,
        grid=(n // tile,),
        in_specs=[pl.BlockSpec((tile, N_HEADS * A_HEAD_DIM), lambda i: (i, 0)),
                  pl.BlockSpec((tile, D_MODEL), lambda i: (i, 0))] + [full(a) for a in weights],
        out_specs=out_specs,
        out_shape=out_shape,
        compiler_params=_params("parallel"),
        name="attn_out",
    )(o, x, *weights)


def _sgu_kernel(x_ref, g_ref, win_ref, lng_ref, lnb_ref, ws_ref, bs_ref, wout_ref, gf_ref, wrt_ref,
                x1_ref, xn_ref, aff_ref):
    x = x_ref[...]
    tile = x.shape[0]
    z = jax.nn.gelu(_dot(_rms(x, g_ref[...]), win_ref[...]))
    u, v = z[:, :C_WIDTH], z[:, C_WIDTH:]
    mu = jnp.mean(v, axis=-1, keepdims=True)
    vc = v - mu
    v = vc * lax.rsqrt(jnp.mean(vc * vc, axis=-1, keepdims=True) + EPS) * lng_ref[...] + lnb_ref[...]
    v = v.astype(BF16)
    bias = bs_ref[...]
    rows = []
    for c in range(tile // C_CHUNK):
        vch = v[c * C_CHUNK:(c + 1) * C_CHUNK, :]
        cols = [_dot(ws_ref[g], vch[:, g * C_GROUP_W:(g + 1) * C_GROUP_W]) for g in range(C_GROUPS)]
        rows.append(jnp.concatenate(cols, axis=1) + bias)
    s = jnp.concatenate(rows, axis=0)
    x1 = x + _dot(u * s, wout_ref[...])
    x1_ref[...] = x1
    _router_tail(x1, gf_ref, wrt_ref, xn_ref, aff_ref)


def _sgu(x, lw, fw, tile):
    n = x.shape[0]
    full = lambda a: pl.BlockSpec(a.shape, lambda i: (0,) * a.ndim)
    out_specs, out_shape = _tail_specs(n, tile)
    weights = (lw["g"], lw["w_in"], lw["ln_g"], lw["ln_b"], lw["w_s"], lw["b_s"], lw["w_out"],
               fw["g"], fw["wrt"])
    return pl.pallas_call(
        _sgu_kernel,
        grid=(n // tile,),
        in_specs=[pl.BlockSpec((tile, D_MODEL), lambda i: (i, 0))] + [full(a) for a in weights],
        out_specs=out_specs,
        out_shape=out_shape,
        compiler_params=_params("parallel"),
        name="sgu",
    )(x, *weights)


def _threshold_kernel(aff_ref, thr_ref, need_ref, *, cap):
    def count(pred):
        return jnp.sum(jnp.where(pred, 1.0, 0.0), axis=1, keepdims=True)

    def body(i, cur):
        cand = cur | jnp.left_shift(jnp.int32(1), 30 - i)
        bits = lax.bitcast_convert_type(aff_ref[...], jnp.int32)
        return jnp.where(count(bits >= cand) >= cap, cand, cur)

    thr = lax.fori_loop(0, 31, body, jnp.zeros((N_EXPERTS, 1), jnp.int32))
    bits = lax.bitcast_convert_type(aff_ref[...], jnp.int32)
    thr_ref[...] = thr
    need_ref[...] = cap - count(bits > thr)


def _threshold(aff_t, cap):
    n = aff_t.shape[1]
    small = lambda dt: (pl.BlockSpec((N_EXPERTS, 1), lambda i: (0, 0)),
                        jax.ShapeDtypeStruct((N_EXPERTS, 1), dt))
    (s0, o0), (s1, o1) = small(jnp.int32), small(F32)
    return pl.pallas_call(
        functools.partial(_threshold_kernel, cap=cap),
        grid=(1,),
        in_specs=[pl.BlockSpec((N_EXPERTS, n), lambda i: (0, 0))],
        out_specs=[s0, s1],
        out_shape=[o0, o1],
        compiler_params=_params("arbitrary"),
        name="ec_threshold",
    )(aff_t)


def _compact_kernel(aff_ref, thr_ref, need_ref, slot_ref, offs_ref, carry_ref, tie_ref, *, chunks, step_tokens):
    step = pl.program_id(0)

    @pl.when(step == 0)
    def _():
        slot_ref[...] = jnp.zeros_like(slot_ref)
        carry_ref[...] = jnp.zeros_like(carry_ref)
        tie_ref[...] = jnp.zeros_like(tie_ref)

    offs_ref[0] = jnp.broadcast_to(carry_ref[...], (N_EXPERTS, LANE))

    row_i = lax.broadcasted_iota(jnp.int32, (SEL_BLOCK, SEL_BLOCK), 0)
    col_i = lax.broadcasted_iota(jnp.int32, (SEL_BLOCK, SEL_BLOCK), 1)
    upper = jnp.where(row_i < col_i, 1.0, 0.0).astype(BF16)
    lane = lax.broadcasted_iota(jnp.int32, (N_EXPERTS, SEL_BLOCK), 1)
    shifts = [1 << b for b in range(SEL_BLOCK.bit_length() - 1)]
    thr = thr_ref[...]
    need = need_ref[...]

    for sb in range(step_tokens // SEL_BLOCK):
        aff = aff_ref[:, sb * SEL_BLOCK:(sb + 1) * SEL_BLOCK]
        bits = lax.bitcast_convert_type(aff, jnp.int32)
        eq = bits == thr
        eq_f = jnp.where(eq, 1.0, 0.0)
        ties_before = jnp.dot(eq_f.astype(BF16), upper, preferred_element_type=F32) + tie_ref[...]
        sel = (bits > thr) | (eq & (ties_before < need))
        sel_f = jnp.where(sel, 1.0, 0.0)
        tie_ref[...] = tie_ref[...] + jnp.sum(eq_f, axis=1, keepdims=True)

        before = jnp.dot(sel_f.astype(BF16), upper, preferred_element_type=F32)
        count = jnp.sum(sel_f, axis=1, keepdims=True).astype(jnp.int32)
        carry = carry_ref[...]
        tok = (step * step_tokens + sb * SEL_BLOCK + lane).astype(F32)

        dist = jnp.where(sel, lane - before.astype(jnp.int32), 0)
        ids = jnp.where(sel, tok, 0.0)
        gates = jnp.where(sel, aff, 0.0)
        for sh in shifts:
            move = (dist & sh) != 0
            step_left = lambda x, zero: (jnp.where(move, zero, x)
                                         + pltpu.roll(jnp.where(move, x, zero), SEL_BLOCK - sh, 1))
            ids, gates, dist = step_left(ids, 0.0), step_left(gates, 0.0), step_left(dist, 0)

        off = carry & (SEL_BLOCK - 1)
        for sh in shifts:
            turn = (off & sh) != 0
            ids = jnp.where(turn, pltpu.roll(ids, sh, 1), ids)
            gates = jnp.where(turn, pltpu.roll(gates, sh, 1), gates)
        end = off + count
        in_lo = (lane >= off) & (lane < end)
        in_hi = lane < end - SEL_BLOCK
        pick = lambda m, x, e: jnp.where(m, x, 0.0)[e:e + 1, :]

        for e in range(N_EXPERTS):
            row = e * chunks + carry[e, 0] // SEL_BLOCK
            lo = jnp.concatenate([pick(in_lo, ids, e), pick(in_lo, gates, e)], axis=0)
            hi = jnp.concatenate([pick(in_hi, ids, e), pick(in_hi, gates, e)], axis=0)
            slot_ref[row] = slot_ref[row] + lo
            slot_ref[row + 1] = slot_ref[row + 1] + hi

        carry_ref[...] = carry + count


def _compact(aff_t, thr, need, chunks, step_tokens):
    n = aff_t.shape[1]
    n_steps = n // step_tokens
    n_rows = N_EXPERTS * chunks + 2
    small = lambda a: pl.BlockSpec(a.shape, lambda i: (0, 0))
    return pl.pallas_call(
        functools.partial(_compact_kernel, chunks=chunks, step_tokens=step_tokens),
        grid=(n_steps,),
        in_specs=[pl.BlockSpec((N_EXPERTS, step_tokens), lambda i: (0, i)), small(thr), small(need)],
        out_specs=[pl.BlockSpec((n_rows, 2, SEL_BLOCK), lambda i: (0, 0, 0)),
                   pl.BlockSpec((1, N_EXPERTS, LANE), lambda i: (i, 0, 0))],
        out_shape=[jax.ShapeDtypeStruct((n_rows, 2, SEL_BLOCK), F32),
                   jax.ShapeDtypeStruct((n_steps, N_EXPERTS, LANE), jnp.int32)],
        scratch_shapes=[pltpu.VMEM((N_EXPERTS, 1), jnp.int32), pltpu.VMEM((N_EXPERTS, 1), F32)],
        compiler_params=_params("arbitrary"),
        name="ec_compact",
    )(aff_t, thr, need)


def _ffn_kernel(idx_ref, gate_ref, xn_hbm, wg_ref, wu_ref, wd_ref, y_ref, buf0, buf1, sem, *, rows, n_pairs):
    pair = pl.program_id(0) * pl.num_programs(1) + pl.program_id(1)

    def row_copy(tok, buf, r, s):
        return pltpu.make_async_copy(xn_hbm.at[pl.ds(tok, 1), :], buf.at[pl.ds(r, 1), :], sem.at[s])

    def gather(chunk, buf, s):
        for r in range(rows):
            row_copy(idx_ref[chunk * rows + r], buf, r, s).start()

    def wait(buf, s):
        pltpu.make_async_copy(xn_hbm.at[pl.ds(0, rows), :], buf, sem.at[s]).wait()

    def compute(buf, half):
        xg = buf[...].astype(BF16)
        hid = jax.nn.silu(_dot(xg, wg_ref[0, 0])) * _dot(xg, wu_ref[0, 0])
        y = _dot(hid, wd_ref[0, 0]) * gate_ref[half * rows:(half + 1) * rows, :]
        y_ref[half * rows:(half + 1) * rows] = y.reshape(rows, 1, D_MODEL)

    @pl.when(pair == 0)
    def _():
        def body(r, c):
            row_copy(idx_ref[r], buf0, r, 0).start()
            return c
        lax.fori_loop(0, rows, body, 0)

    wait(buf0, 0)
    gather(2 * pair + 1, buf1, 1)
    compute(buf0, 0)
    wait(buf1, 1)
    gather(jnp.where(pair + 1 < n_pairs, 2 * pair + 2, 0), buf0, 0)
    compute(buf1, 1)

    @pl.when(pair == n_pairs - 1)
    def _():
        wait(buf0, 0)


def _ffn(xn, idx, gate, fw, cap, rows):
    pairs = cap // (2 * rows)
    layer = fw["layer"]
    w_spec = lambda shape: pl.BlockSpec((1, 1) + shape, lambda e, c, idx: (layer, e, 0, 0))
    grid_spec = pltpu.PrefetchScalarGridSpec(
        num_scalar_prefetch=1,
        grid=(N_EXPERTS, pairs),
        in_specs=[pl.BlockSpec((2 * rows, 1), lambda e, c, idx: (e * pairs + c, 0)),
                  pl.BlockSpec(memory_space=pl.ANY),
                  w_spec((D_MODEL, EXPERT_FF)), w_spec((D_MODEL, EXPERT_FF)), w_spec((EXPERT_FF, D_MODEL))],
        out_specs=pl.BlockSpec((2 * rows, 1, D_MODEL), lambda e, c, idx: (e * pairs + c, 0, 0)),
        scratch_shapes=[pltpu.VMEM((rows, D_MODEL), F32), pltpu.VMEM((rows, D_MODEL), F32),
                        pltpu.SemaphoreType.DMA((2,))],
    )
    return pl.pallas_call(
        functools.partial(_ffn_kernel, rows=rows, n_pairs=N_EXPERTS * pairs),
        grid_spec=grid_spec,
        out_shape=jax.ShapeDtypeStruct((N_EXPERTS * cap, 1, D_MODEL), F32),
        compiler_params=pltpu.CompilerParams(dimension_semantics=("arbitrary", "arbitrary"),
                                             vmem_limit_bytes=VMEM_LIMIT, disable_bounds_checks=True),
        name="ec_ffn",
    )(idx, gate, xn, fw["w_gate"], fw["w_up"], fw["w_down"])


def _combine_kernel(offs_ref, idx_ref, x1_ref, y_hbm, *rest, cap, n_tiles, tile, width, normalize):
    norm_g, (o_ref, acc, ywin, yover, sem, osem) = (rest[0], rest[1:]) if normalize else (None, rest)
    j = pl.program_id(0)
    slot = j % 2
    total = N_EXPERTS * cap

    def window(t, e):
        first = e * cap + offs_ref[t * N_EXPERTS + e]
        return first, jnp.minimum(first, total - width)

    def win_copy(t, e, s):
        return pltpu.make_async_copy(y_hbm.at[pl.ds(window(t, e)[1], width)], ywin.at[s, e], sem.at[s])

    def fetch(t, s):
        for e in range(N_EXPERTS):
            win_copy(t, e, s).start()

    def add_rows(win, first, shift, count):
        def group(g, partial):
            work = []
            for u in range(ROW_UNROLL):
                r = g * ROW_UNROLL + u
                if partial:
                    tok = idx_ref[jnp.minimum(first + r, total - 1)] - j * tile
                    dst = jnp.where(r < count, tok, tile + u)
                    src = jnp.minimum(shift + r, width - 1)
                else:
                    dst = idx_ref[first + r] - j * tile
                    src = shift + r
                work.append((dst, acc[dst] + win[src]))
            for dst, val in work:
                acc[dst] = val

        def body(g, c):
            group(g, False)
            return c
        whole = count // ROW_UNROLL
        lax.fori_loop(0, whole, body, 0)
        group(whole, True)

    @pl.when(j == 0)
    def _():
        fetch(0, 0)

    @pl.when(j + 1 < n_tiles)
    def _():
        fetch(j + 1, 1 - slot)

    acc[...] = jnp.zeros_like(acc)
    for e in range(N_EXPERTS):
        win_copy(j, e, slot).wait()

    for e in range(N_EXPERTS):
        first, start = window(j, e)
        count = offs_ref[(j + 1) * N_EXPERTS + e] - offs_ref[j * N_EXPERTS + e]
        add_rows(ywin.at[slot, e], first, first - start, jnp.minimum(count, width))

        def overflow(k, c, first=first, count=count):
            begin = first + k * width
            start = jnp.minimum(begin, total - width)
            cp = pltpu.make_async_copy(y_hbm.at[pl.ds(start, width)], yover, osem)
            cp.start()
            cp.wait()
            add_rows(yover, begin, begin - start, jnp.minimum(count - k * width, width))
            return c
        lax.fori_loop(1, (count + width - 1) // width, overflow, 0)

    out = x1_ref[...] + acc[0:tile].reshape(tile, D_MODEL)
    o_ref[...] = _rms(out, norm_g[...]) if normalize else out


def _combine(x1, y, idx, offs, cap, tile, width, final_g):
    extra = () if final_g is None else (final_g,)
    n = x1.shape[0]
    n_tiles = n // tile
    grid_spec = pltpu.PrefetchScalarGridSpec(
        num_scalar_prefetch=2,
        grid=(n_tiles,),
        in_specs=[pl.BlockSpec((tile, D_MODEL), lambda i, offs, idx: (i, 0)),
                  pl.BlockSpec(memory_space=pl.ANY)]
                 + [pl.BlockSpec((1, D_MODEL), lambda i, offs, idx: (0, 0)) for _ in extra],
        out_specs=pl.BlockSpec((tile, D_MODEL), lambda i, offs, idx: (i, 0)),
        scratch_shapes=[pltpu.VMEM((tile + ROW_UNROLL, 1, D_MODEL), F32),
                        pltpu.VMEM((2, N_EXPERTS, width, 1, D_MODEL), F32),
                        pltpu.VMEM((width, 1, D_MODEL), F32),
                        pltpu.SemaphoreType.DMA((2,)), pltpu.SemaphoreType.DMA(())],
    )
    return pl.pallas_call(
        functools.partial(_combine_kernel, cap=cap, n_tiles=n_tiles, tile=tile, width=width,
                          normalize=final_g is not None),
        grid_spec=grid_spec,
        out_shape=jax.ShapeDtypeStruct((n, D_MODEL), F32),
        compiler_params=_params("arbitrary"),
        name="ec_combine",
    )(offs, idx, x1, y, *extra)


def _ec_ffn(x1, xn, aff_t, fw, final_g):
    n = x1.shape[0]
    cap = EC_CAPACITY_FACTOR * n // N_EXPERTS
    chunks = cap // SEL_BLOCK
    step_tokens = min(SEL_STEP, n)
    thr, need = _threshold(aff_t, cap)
    slots, offs_t = _compact(aff_t, thr, need, chunks, step_tokens)
    idx = slots[:N_EXPERTS * chunks, 0, :].reshape(-1).astype(jnp.int32)
    gate = slots[:N_EXPERTS * chunks, 1, :].reshape(-1, 1)
    offs = jnp.concatenate([offs_t[:, :, 0], jnp.full((1, N_EXPERTS), cap, jnp.int32)], axis=0).reshape(-1)
    y = _ffn(xn, idx, gate, fw, cap, min(FFN_ROWS, cap // 2))
    return _combine(x1, y, idx, offs, cap, step_tokens, min(COMBINE_W, cap), final_g)


def _pad_heads(w, n_heads, width):
    lead = w.shape[:-1]
    w = w.reshape(lead + (n_heads, width))
    w = jnp.pad(w, [(0, 0)] * len(lead) + [(0, 0), (0, LANE - width)])
    return w.reshape(lead + (n_heads * LANE,))


def _rope_tables(seq):
    rows = seq // GRID_W
    row = jnp.repeat(jnp.arange(rows, dtype=F32), GRID_W)
    col = jnp.tile(jnp.arange(GRID_W, dtype=F32), rows)

    def cos_sin(rot_dim):
        n_pair = rot_dim // 4
        freq = ROPE_THETA ** (-jnp.arange(n_pair, dtype=F32) / n_pair)
        ang = jnp.concatenate([row[:, None] * freq, col[:, None] * freq], axis=-1)
        return jnp.cos(ang), jnp.sin(ang)

    def place(parts):
        out = jnp.zeros((seq, LANE), F32)
        for start, val in parts:
            out = out.at[:, start:start + val.shape[1]].set(val)
        return out

    ca, sa = cos_sin(A_HEAD_DIM)
    cb, sb = cos_sin(B_ROPE)
    ha, hb = A_HEAD_DIM // 2, B_ROPE // 2
    ones = jnp.ones((seq, B_NOPE), F32)
    return (place([(0, ca), (ha, ca)]), place([(0, -sa)]), place([(ha, sa)]),
            place([(0, ones), (B_NOPE, cb), (B_NOPE + hb, cb)]), place([(B_NOPE, -sb)]),
            place([(B_NOPE + hb, sb)]))


def _prep_attn_layer(i, p, rope):
    w_in = p["w_in_attn"][i]
    o = 0
    w_qa = w_in[:, o:o + A_Q_W]; o += A_Q_W
    w_ka = w_in[:, o:o + A_KV_W]; o += A_KV_W
    w_va = w_in[:, o:o + A_KV_W]; o += A_KV_W
    w_cq = w_in[:, o:o + B_Q_RANK]; o += B_Q_RANK
    w_ckv = w_in[:, o:o + B_KV_RANK]; o += B_KV_RANK
    w_kr = w_in[:, o:o + B_ROPE]
    w_kr = jnp.pad(w_kr, [(0, 0), (B_NOPE, LANE - B_NOPE - B_ROPE)])
    w_all = jnp.concatenate([_pad_heads(w_qa, A_HEADS, A_HEAD_DIM), _pad_heads(w_ka, A_KV_HEADS, A_HEAD_DIM),
                             _pad_heads(w_va, A_KV_HEADS, A_HEAD_DIM), w_cq, w_ckv, w_kr], axis=1)
    w_ukv = p["w_ukv"][i].reshape(B_KV_RANK, B_HEADS, B_NOPE + B_V)
    w_uk = _pad_heads(w_ukv[:, :, :B_NOPE].reshape(B_KV_RANK, -1), B_HEADS, B_NOPE)
    w_uv = _pad_heads(w_ukv[:, :, B_NOPE:].reshape(B_KV_RANK, -1), B_HEADS, B_V)
    return {
        "g": p["attn_norm"][i][None, :],
        "w_all": w_all.astype(BF16),
        "gq": jnp.tile(jnp.pad(p["qk_norm_q"][i], (0, LANE - A_HEAD_DIM)), A_HEADS)[None, :],
        "gk": jnp.tile(jnp.pad(p["qk_norm_k"][i], (0, LANE - A_HEAD_DIM)), A_KV_HEADS)[None, :],
        "mq": p["mla_q_norm"][i][None, :],
        "wuq": _pad_heads(p["w_uq"][i], B_HEADS, B_QK).astype(BF16),
        "mkv": p["mla_kv_norm"][i][None, :],
        "wukv": jnp.concatenate([w_uk, w_uv], axis=1).astype(BF16),
        "w_out": p["w_out_attn"][i].astype(BF16),
        "rope": rope,
    }


def _prep_sgu_layer(i, p):
    b_s = p["b_spatial"][i]
    return {
        "g": p["sgu_norm"][i][None, :],
        "w_in": p["w_in_sgu"][i].astype(BF16),
        "ln_g": p["sgu_ln_g"][i][None, :],
        "ln_b": p["sgu_ln_b"][i][None, :],
        "w_s": p["w_spatial"][i].astype(BF16),
        "b_s": jnp.repeat(b_s.T, C_GROUP_W, axis=1),
        "w_out": p["w_out_sgu"][i].astype(BF16),
    }


def _prep_ffn_layer(l, p, w_gate, w_up, w_down):
    return {
        "g": p["ffn_norm"][l][None, :],
        "wrt": p["w_router"][l].T.astype(BF16),
        "layer": l, "w_gate": w_gate, "w_up": w_up, "w_down": w_down,
    }


def _pick(n, target):
    t = min(n, target)
    while n % t:
        t //= 2
    return t


def _trunk(x, attn_layers, sgu_layers, ffn_layers, final_g):
    batch, seq, _ = x.shape
    n = batch * seq
    x = x.reshape(n, D_MODEL)
    tile = _pick(seq, 512)
    tq = _pick(seq, 512)
    for l, fw in enumerate(ffn_layers):
        if l % 2 == 0:
            lw = attn_layers[l // 2]
            q, k, v = _attn_proj(x, lw, seq, tile)
            o = _attention(q, k, v, batch, seq, tq)
            x1, xn, aff_t = _attn_out(o, x, lw, fw, tile)
        else:
            x1, xn, aff_t = _sgu(x, sgu_layers[l // 2], fw, tile)
        last = l == len(ffn_layers) - 1
        x = _ec_ffn(x1, xn, aff_t, fw, final_g[None, :] if last else None)
    return x.reshape(batch, seq, D_MODEL)


def kernel(x_prompt, x_sample, attn_norm, w_in_attn, qk_norm_q, qk_norm_k, mla_q_norm, w_uq, mla_kv_norm, w_ukv, w_out_attn, sgu_norm, w_in_sgu, sgu_ln_g, sgu_ln_b, w_spatial, b_spatial, w_out_sgu, ffn_norm, w_router, w_gate, w_up, w_down, final_norm):
    p = dict(attn_norm=attn_norm, w_in_attn=w_in_attn, qk_norm_q=qk_norm_q, qk_norm_k=qk_norm_k,
             mla_q_norm=mla_q_norm, w_uq=w_uq, mla_kv_norm=mla_kv_norm, w_ukv=w_ukv,
             w_out_attn=w_out_attn, sgu_norm=sgu_norm, w_in_sgu=w_in_sgu, sgu_ln_g=sgu_ln_g,
             sgu_ln_b=sgu_ln_b, w_spatial=w_spatial, b_spatial=b_spatial, w_out_sgu=w_out_sgu,
             ffn_norm=ffn_norm, w_router=w_router)
    depth = ffn_norm.shape[0]
    wg, wu, wd = w_gate.astype(BF16), w_up.astype(BF16), w_down.astype(BF16)
    ffn_layers = [_prep_ffn_layer(l, p, wg, wu, wd) for l in range(depth)]
    sgu_layers = [_prep_sgu_layer(i, p) for i in range(sgu_norm.shape[0])]
    outs = []
    for x in (x_prompt, x_sample):
        rope = _rope_tables(x.shape[1])
        attn_layers = [_prep_attn_layer(i, p, rope) for i in range(attn_norm.shape[0])]
        outs.append(_trunk(x, attn_layers, sgu_layers, ffn_layers, final_norm))
    return tuple(outs)
```

```python
import functools

import jax
import jax.numpy as jnp
from jax import lax
from jax.experimental import pallas as pl
from jax.experimental.pallas import tpu as pltpu

D_MODEL = 1024
GRID_W = 64
ROPE_THETA = 10000.0
EPS = 1e-6

A_HEADS = 8
A_KV_HEADS = 2
A_GROUP = A_HEADS // A_KV_HEADS
A_HEAD_DIM = 64
A_Q_W = A_HEADS * A_HEAD_DIM
A_KV_W = A_KV_HEADS * A_HEAD_DIM

B_HEADS = 8
B_Q_RANK = 384
B_KV_RANK = 256
B_NOPE = 64
B_ROPE = 32
B_V = 64
B_QK = B_NOPE + B_ROPE

N_HEADS = A_HEADS + B_HEADS
HEADS_PER_STEP = 4
LOG2_E = 1.4426950408889634

C_WIDTH = D_MODEL
C_GROUPS = 8
C_GROUP_W = C_WIDTH // C_GROUPS
C_CHUNK = 128

N_EXPERTS = 16
EXPERT_FF = 2 * D_MODEL
EC_CAPACITY_FACTOR = 2

LANE = 128
VMEM_LIMIT = 56 * 1024 * 1024
SEL_BLOCK = LANE
SEL_STEP = 1024
FFN_ROWS = 512
COMBINE_W = 176
ROW_UNROLL = 8
BF16 = jnp.bfloat16
F32 = jnp.float32


def _params(*sem):
    return pltpu.CompilerParams(dimension_semantics=sem, vmem_limit_bytes=VMEM_LIMIT)


def _dot(a, b):
    return jnp.dot(a.astype(BF16), b.astype(BF16), preferred_element_type=F32)


def _dot_nt(a, b):
    return lax.dot_general(a.astype(BF16), b.astype(BF16), (((1,), (1,)), ((), ())),
                           preferred_element_type=F32)


def _rms(x, g):
    return x * lax.rsqrt(jnp.mean(x * x, axis=-1, keepdims=True) + EPS) * g


def _router_tail(x1, g_ref, wrt_ref, xn_ref, aff_ref):
    xn = _rms(x1, g_ref[...])
    xn_ref[...] = xn
    logits = _dot_nt(wrt_ref[...], xn)
    m = jnp.max(logits, axis=0, keepdims=True)
    e = jnp.exp(logits - m)
    aff_ref[...] = e / jnp.sum(e, axis=0, keepdims=True)


def _head_norm(z, n_heads, width):
    outs = []
    for h in range(n_heads):
        blk = z[:, h * LANE:(h + 1) * LANE]
        ms = jnp.sum(blk * blk, axis=-1, keepdims=True) * (1.0 / width)
        outs.append(blk * lax.rsqrt(ms + EPS))
    return jnp.concatenate(outs, axis=1)


def _rope(x, c, s_lo, s_hi, half, n_heads):
    w = x.shape[1]
    c = jnp.tile(c, (1, n_heads))
    s_lo = jnp.tile(s_lo, (1, n_heads))
    s_hi = jnp.tile(s_hi, (1, n_heads))
    return x * c + pltpu.roll(x, w - half, 1) * s_lo + pltpu.roll(x, half, 1) * s_hi


def _attn_proj_kernel(x_ref, g_ref, w_ref, gq_ref, gk_ref, mq_ref, wuq_ref, mkv_ref, wukv_ref,
                      ca_ref, sa1_ref, sa2_ref, cb_ref, sb1_ref, sb2_ref,
                      q_ref, k_ref, v_ref):
    h = _rms(x_ref[...], g_ref[...])
    z = _dot(h, w_ref[...])
    o = 0
    qa = z[:, o:o + A_HEADS * LANE]; o += A_HEADS * LANE
    ka = z[:, o:o + A_KV_HEADS * LANE]; o += A_KV_HEADS * LANE
    va = z[:, o:o + A_KV_HEADS * LANE]; o += A_KV_HEADS * LANE
    cq = z[:, o:o + B_Q_RANK]; o += B_Q_RANK
    ckv = z[:, o:o + B_KV_RANK]; o += B_KV_RANK
    kr = z[:, o:o + LANE]

    ca, sa1, sa2 = ca_ref[...], sa1_ref[...], sa2_ref[...]
    cb, sb1, sb2 = cb_ref[...], sb1_ref[...], sb2_ref[...]

    qa = _rope(_head_norm(qa, A_HEADS, A_HEAD_DIM) * gq_ref[...], ca, sa1, sa2, A_HEAD_DIM // 2, A_HEADS)
    ka = _rope(_head_norm(ka, A_KV_HEADS, A_HEAD_DIM) * gk_ref[...], ca, sa1, sa2, A_HEAD_DIM // 2,
               A_KV_HEADS)

    qb = _rope(_dot(_rms(cq, mq_ref[...]), wuq_ref[...]), cb, sb1, sb2, B_ROPE // 2, B_HEADS)
    kvb = _dot(_rms(ckv, mkv_ref[...]), wukv_ref[...])
    kr = _rope(kr, cb, sb1, sb2, B_ROPE // 2, 1)
    kb = kvb[:, :B_HEADS * LANE] + jnp.tile(kr, (1, B_HEADS))
    vb = kvb[:, B_HEADS * LANE:]

    q_ref[...] = jnp.concatenate([qa, qb], axis=1).astype(BF16)
    per_query_head = lambda x: [x[:, g * LANE:(g + 1) * LANE] for g in range(A_KV_HEADS)
                                for _ in range(A_GROUP)]
    k_ref[...] = jnp.concatenate(per_query_head(ka) + [kb], axis=1).astype(BF16)
    v = jnp.concatenate(per_query_head(va) + [vb], axis=1)
    lane = lax.broadcasted_iota(jnp.int32, v.shape, 1)
    v_ref[...] = jnp.where(lane % LANE == A_HEAD_DIM, 1.0, v).astype(BF16)


def _attn_proj(x, lw, seq, tile):
    n = x.shape[0]
    nt = n // tile
    per_seq = seq // tile
    full = lambda a: pl.BlockSpec(a.shape, lambda i: (0,) * a.ndim)
    tok = lambda w: pl.BlockSpec((tile, w), lambda i: (i, 0))
    tab = pl.BlockSpec((tile, LANE), lambda i: (i % per_seq, 0))
    weights = (lw["g"], lw["w_all"], lw["gq"], lw["gk"], lw["mq"], lw["wuq"], lw["mkv"], lw["wukv"])
    return pl.pallas_call(
        _attn_proj_kernel,
        grid=(nt,),
        in_specs=[tok(D_MODEL)] + [full(a) for a in weights] + [tab] * 6,
        out_specs=[tok(N_HEADS * LANE)] * 3,
        out_shape=[jax.ShapeDtypeStruct((n, N_HEADS * LANE), BF16)] * 3,
        compiler_params=_params("parallel"),
        name="attn_proj",
    )(x, *weights, *lw["rope"])


def _attn_kernel(q_ref, k_ref, v_ref, o_ref):
    pair = pl.program_id(1)
    scale = jnp.where(pair < A_HEADS // HEADS_PER_STEP, A_HEAD_DIM ** -0.5, B_QK ** -0.5).astype(F32)
    c = scale * LOG2_E
    heads = [slice(h * LANE, (h + 1) * LANE) for h in range(HEADS_PER_STEP)]
    scores = [_dot_nt(q_ref[:, sl], k_ref[:, sl]) for sl in heads]
    outs = []
    for s, sl in zip(scores, heads):
        m = jnp.max(s, axis=-1, keepdims=True)
        p = jnp.exp2((s - m) * c)
        o = _dot(p, v_ref[:, sl])
        outs.append((o / o[:, A_HEAD_DIM:A_HEAD_DIM + 1])[:, :A_HEAD_DIM])
    o_ref[...] = jnp.concatenate(outs, axis=1).astype(BF16)


def _attention(q, k, v, batch, seq, tq):
    n = q.shape[0]
    nq = seq // tq
    width = HEADS_PER_STEP * LANE
    return pl.pallas_call(
        _attn_kernel,
        grid=(batch, N_HEADS // HEADS_PER_STEP, nq),
        in_specs=[pl.BlockSpec((tq, width), lambda b, h, i: (b * nq + i, h)),
                  pl.BlockSpec((seq, width), lambda b, h, i: (b, h)),
                  pl.BlockSpec((seq, width), lambda b, h, i: (b, h))],
        out_specs=pl.BlockSpec((tq, HEADS_PER_STEP * A_HEAD_DIM), lambda b, h, i: (b * nq + i, h)),
        out_shape=jax.ShapeDtypeStruct((n, N_HEADS * A_HEAD_DIM), BF16),
        compiler_params=_params("parallel", "parallel", "parallel"),
        name="attention",
    )(q, k, v)


def _attn_out_kernel(o_ref, x_ref, w_ref, g_ref, wrt_ref, x1_ref, xn_ref, aff_ref):
    x1 = x_ref[...] + _dot(o_ref[...], w_ref[...])
    x1_ref[...] = x1
    _router_tail(x1, g_ref, wrt_ref, xn_ref, aff_ref)


def _tail_specs(n, tile):
    out_specs = [pl.BlockSpec((tile, D_MODEL), lambda i: (i, 0)),
                 pl.BlockSpec((tile, D_MODEL), lambda i: (i, 0)),
                 pl.BlockSpec((N_EXPERTS, tile), lambda i: (0, i))]
    out_shape = [jax.ShapeDtypeStruct((n, D_MODEL), F32),
                 jax.ShapeDtypeStruct((n, D_MODEL), F32),
                 jax.ShapeDtypeStruct((N_EXPERTS, n), F32)]
    return out_specs, out_shape


def _attn_out(o, x, lw, fw, tile):
    n = x.shape[0]
    full = lambda a: pl.BlockSpec(a.shape, lambda i: (0,) * a.ndim)
    out_specs, out_shape = _tail_specs(n, tile)
    weights = (lw["w_out"], fw["g"], fw["wrt"])
    return pl.pallas_call(
        _attn_out_kernel,
        grid=(n // tile,),
        in_specs=[pl.BlockSpec((tile, N_HEADS * A_HEAD_DIM), lambda i: (i, 0)),
                  pl.BlockSpec((tile, D_MODEL), lambda i: (i, 0))] + [full(a) for a in weights],
        out_specs=out_specs,
        out_shape=out_shape,
        compiler_params=_params("parallel"),
        name="attn_out",
    )(o, x, *weights)


def _sgu_kernel(x_ref, g_ref, win_ref, lng_ref, lnb_ref, ws_ref, bs_ref, wout_ref, gf_ref, wrt_ref,
                x1_ref, xn_ref, aff_ref):
    x = x_ref[...]
    tile = x.shape[0]
    z = jax.nn.gelu(_dot(_rms(x, g_ref[...]), win_ref[...]))
    u, v = z[:, :C_WIDTH], z[:, C_WIDTH:]
    mu = jnp.mean(v, axis=-1, keepdims=True)
    vc = v - mu
    v = vc * lax.rsqrt(jnp.mean(vc * vc, axis=-1, keepdims=True) + EPS) * lng_ref[...] + lnb_ref[...]
    v = v.astype(BF16)
    bias = bs_ref[...]
    rows = []
    for c in range(tile // C_CHUNK):
        vch = v[c * C_CHUNK:(c + 1) * C_CHUNK, :]
        cols = [_dot(ws_ref[g], vch[:, g * C_GROUP_W:(g + 1) * C_GROUP_W]) for g in range(C_GROUPS)]
        rows.append(jnp.concatenate(cols, axis=1) + bias)
    s = jnp.concatenate(rows, axis=0)
    x1 = x + _dot(u * s, wout_ref[...])
    x1_ref[...] = x1
    _router_tail(x1, gf_ref, wrt_ref, xn_ref, aff_ref)


def _sgu(x, lw, fw, tile):
    n = x.shape[0]
    full = lambda a: pl.BlockSpec(a.shape, lambda i: (0,) * a.ndim)
    out_specs, out_shape = _tail_specs(n, tile)
    weights = (lw["g"], lw["w_in"], lw["ln_g"], lw["ln_b"], lw["w_s"], lw["b_s"], lw["w_out"],
               fw["g"], fw["wrt"])
    return pl.pallas_call(
        _sgu_kernel,
        grid=(n // tile,),
        in_specs=[pl.BlockSpec((tile, D_MODEL), lambda i: (i, 0))] + [full(a) for a in weights],
        out_specs=out_specs,
        out_shape=out_shape,
        compiler_params=_params("parallel"),
        name="sgu",
    )(x, *weights)


def _threshold_kernel(aff_ref, thr_ref, need_ref, *, cap):
    def count(pred):
        return jnp.sum(jnp.where(pred, 1.0, 0.0), axis=1, keepdims=True)

    def body(i, cur):
        cand = cur | jnp.left_shift(jnp.int32(1), 30 - i)
        bits = lax.bitcast_convert_type(aff_ref[...], jnp.int32)
        return jnp.where(count(bits >= cand) >= cap, cand, cur)

    thr = lax.fori_loop(0, 31, body, jnp.zeros((N_EXPERTS, 1), jnp.int32))
    bits = lax.bitcast_convert_type(aff_ref[...], jnp.int32)
    thr_ref[...] = thr
    need_ref[...] = cap - count(bits > thr)


def _threshold(aff_t, cap):
    n = aff_t.shape[1]
    small = lambda dt: (pl.BlockSpec((N_EXPERTS, 1), lambda i: (0, 0)),
                        jax.ShapeDtypeStruct((N_EXPERTS, 1), dt))
    (s0, o0), (s1, o1) = small(jnp.int32), small(F32)
    return pl.pallas_call(
        functools.partial(_threshold_kernel, cap=cap),
        grid=(1,),
        in_specs=[pl.BlockSpec((N_EXPERTS, n), lambda i: (0, 0))],
        out_specs=[s0, s1],
        out_shape=[o0, o1],
        compiler_params=_params("arbitrary"),
        name="ec_threshold",
    )(aff_t)


def _compact_kernel(aff_ref, thr_ref, need_ref, slot_ref, offs_ref, carry_ref, tie_ref, *, chunks, step_tokens):
    step = pl.program_id(0)

    @pl.when(step == 0)
    def _():
        slot_ref[...] = jnp.zeros_like(slot_ref)
        carry_ref[...] = jnp.zeros_like(carry_ref)
        tie_ref[...] = jnp.zeros_like(tie_ref)

    offs_ref[0] = jnp.broadcast_to(carry_ref[...], (N_EXPERTS, LANE))

    row_i = lax.broadcasted_iota(jnp.int32, (SEL_BLOCK, SEL_BLOCK), 0)
    col_i = lax.broadcasted_iota(jnp.int32, (SEL_BLOCK, SEL_BLOCK), 1)
    upper = jnp.where(row_i < col_i, 1.0, 0.0).astype(BF16)
    lane = lax.broadcasted_iota(jnp.int32, (N_EXPERTS, SEL_BLOCK), 1)
    shifts = [1 << b for b in range(SEL_BLOCK.bit_length() - 1)]
    thr = thr_ref[...]
    need = need_ref[...]

    for sb in range(step_tokens // SEL_BLOCK):
        aff = aff_ref[:, sb * SEL_BLOCK:(sb + 1) * SEL_BLOCK]
        bits = lax.bitcast_convert_type(aff, jnp.int32)
        eq = bits == thr
        eq_f = jnp.where(eq, 1.0, 0.0)
        ties_before = jnp.dot(eq_f.astype(BF16), upper, preferred_element_type=F32) + tie_ref[...]
        sel = (bits > thr) | (eq & (ties_before < need))
        sel_f = jnp.where(sel, 1.0, 0.0)
        tie_ref[...] = tie_ref[...] + jnp.sum(eq_f, axis=1, keepdims=True)

        before = jnp.dot(sel_f.astype(BF16), upper, preferred_element_type=F32)
        count = jnp.sum(sel_f, axis=1, keepdims=True).astype(jnp.int32)
        carry = carry_ref[...]
        tok = (step * step_tokens + sb * SEL_BLOCK + lane).astype(F32)

        dist = jnp.where(sel, lane - before.astype(jnp.int32), 0)
        ids = jnp.where(sel, tok, 0.0)
        gates = jnp.where(sel, aff, 0.0)
        for sh in shifts:
            move = (dist & sh) != 0
            step_left = lambda x, zero: (jnp.where(move, zero, x)
                                         + pltpu.roll(jnp.where(move, x, zero), SEL_BLOCK - sh, 1))
            ids, gates, dist = step_left(ids, 0.0), step_left(gates, 0.0), step_left(dist, 0)

        off = carry & (SEL_BLOCK - 1)
        for sh in shifts:
            turn = (off & sh) != 0
            ids = jnp.where(turn, pltpu.roll(ids, sh, 1), ids)
            gates = jnp.where(turn, pltpu.roll(gates, sh, 1), gates)
        end = off + count
        in_lo = (lane >= off) & (lane < end)
        in_hi = lane < end - SEL_BLOCK
        pick = lambda m, x, e: jnp.where(m, x, 0.0)[e:e + 1, :]

        for e in range(N_EXPERTS):
            row = e * chunks + carry[e, 0] // SEL_BLOCK
            lo = jnp.concatenate([pick(in_lo, ids, e), pick(in_lo, gates, e)], axis=0)
            hi = jnp.concatenate([pick(in_hi, ids, e), pick(in_hi, gates, e)], axis=0)
            slot_ref[row] = slot_ref[row] + lo
            slot_ref[row + 1] = slot_ref[row + 1] + hi

        carry_ref[...] = carry + count


def _compact(aff_t, thr, need, chunks, step_tokens):
    n = aff_t.shape[1]
    n_steps = n // step_tokens
    n_rows = N_EXPERTS * chunks + 2
    small = lambda a: pl.BlockSpec(a.shape, lambda i: (0, 0))
    return pl.pallas_call(
        functools.partial(_compact_kernel, chunks=chunks, step_tokens=step_tokens),
        grid=(n_steps,),
        in_specs=[pl.BlockSpec((N_EXPERTS, step_tokens), lambda i: (0, i)), small(thr), small(need)],
        out_specs=[pl.BlockSpec((n_rows, 2, SEL_BLOCK), lambda i: (0, 0, 0)),
                   pl.BlockSpec((1, N_EXPERTS, LANE), lambda i: (i, 0, 0))],
        out_shape=[jax.ShapeDtypeStruct((n_rows, 2, SEL_BLOCK), F32),
                   jax.ShapeDtypeStruct((n_steps, N_EXPERTS, LANE), jnp.int32)],
        scratch_shapes=[pltpu.VMEM((N_EXPERTS, 1), jnp.int32), pltpu.VMEM((N_EXPERTS, 1), F32)],
        compiler_params=_params("arbitrary"),
        name="ec_compact",
    )(aff_t, thr, need)


def _ffn_kernel(idx_ref, gate_ref, xn_hbm, wg_ref, wu_ref, wd_ref, y_ref, buf0, buf1, sem, *, rows, n_pairs):
    pair = pl.program_id(0) * pl.num_programs(1) + pl.program_id(1)

    def row_copy(tok, buf, r, s):
        return pltpu.make_async_copy(xn_hbm.at[pl.ds(tok, 1), :], buf.at[pl.ds(r, 1), :], sem.at[s])

    def gather(chunk, buf, s):
        for r in range(rows):
            row_copy(idx_ref[chunk * rows + r], buf, r, s).start()

    def wait(buf, s):
        pltpu.make_async_copy(xn_hbm.at[pl.ds(0, rows), :], buf, sem.at[s]).wait()

    def compute(buf, half):
        xg = buf[...].astype(BF16)
        hid = jax.nn.silu(_dot(xg, wg_ref[0, 0])) * _dot(xg, wu_ref[0, 0])
        y = _dot(hid, wd_ref[0, 0]) * gate_ref[half * rows:(half + 1) * rows, :]
        y_ref[half * rows:(half + 1) * rows] = y.reshape(rows, 1, D_MODEL)

    @pl.when(pair == 0)
    def _():
        def body(r, c):
            row_copy(idx_ref[r], buf0, r, 0).start()
            return c
        lax.fori_loop(0, rows, body, 0)

    wait(buf0, 0)
    gather(2 * pair + 1, buf1, 1)
    compute(buf0, 0)
    wait(buf1, 1)
    gather(jnp.where(pair + 1 < n_pairs, 2 * pair + 2, 0), buf0, 0)
    compute(buf1, 1)

    @pl.when(pair == n_pairs - 1)
    def _():
        wait(buf0, 0)


def _ffn(xn, idx, gate, fw, cap, rows):
    pairs = cap // (2 * rows)
    layer = fw["layer"]
    w_spec = lambda shape: pl.BlockSpec((1, 1) + shape, lambda e, c, idx: (layer, e, 0, 0))
    grid_spec = pltpu.PrefetchScalarGridSpec(
        num_scalar_prefetch=1,
        grid=(N_EXPERTS, pairs),
        in_specs=[pl.BlockSpec((2 * rows, 1), lambda e, c, idx: (e * pairs + c, 0)),
                  pl.BlockSpec(memory_space=pl.ANY),
                  w_spec((D_MODEL, EXPERT_FF)), w_spec((D_MODEL, EXPERT_FF)), w_spec((EXPERT_FF, D_MODEL))],
        out_specs=pl.BlockSpec((2 * rows, 1, D_MODEL), lambda e, c, idx: (e * pairs + c, 0, 0)),
        scratch_shapes=[pltpu.VMEM((rows, D_MODEL), F32), pltpu.VMEM((rows, D_MODEL), F32),
                        pltpu.SemaphoreType.DMA((2,))],
    )
    return pl.pallas_call(
        functools.partial(_ffn_kernel, rows=rows, n_pairs=N_EXPERTS * pairs),
        grid_spec=grid_spec,
        out_shape=jax.ShapeDtypeStruct((N_EXPERTS * cap, 1, D_MODEL), F32),
        compiler_params=pltpu.CompilerParams(dimension_semantics=("arbitrary", "arbitrary"),
                                             vmem_limit_bytes=VMEM_LIMIT, disable_bounds_checks=True),
        name="ec_ffn",
    )(idx, gate, xn, fw["w_gate"], fw["w_up"], fw["w_down"])


def _combine_kernel(offs_ref, idx_ref, x1_ref, y_hbm, *rest, cap, n_tiles, tile, width, normalize):
    norm_g, (o_ref, acc, ywin, yover, sem, osem) = (rest[0], rest[1:]) if normalize else (None, rest)
    j = pl.program_id(0)
    slot = j % 2
    total = N_EXPERTS * cap

    def window(t, e):
        first = e * cap + offs_ref[t * N_EXPERTS + e]
        return first, jnp.minimum(first, total - width)

    def win_copy(t, e, s):
        return pltpu.make_async_copy(y_hbm.at[pl.ds(window(t, e)[1], width)], ywin.at[s, e], sem.at[s])

    def fetch(t, s):
        for e in range(N_EXPERTS):
            win_copy(t, e, s).start()

    def add_rows(win, first, shift, count):
        def group(g, partial):
            work = []
            for u in range(ROW_UNROLL):
                r = g * ROW_UNROLL + u
                if partial:
                    tok = idx_ref[jnp.minimum(first + r, total - 1)] - j * tile
                    dst = jnp.where(r < count, tok, tile + u)
                    src = jnp.minimum(shift + r, width - 1)
                else:
                    dst = idx_ref[first + r] - j * tile
                    src = shift + r
                work.append((dst, acc[dst] + win[src]))
            for dst, val in work:
                acc[dst] = val

        def body(g, c):
            group(g, False)
            return c
        whole = count // ROW_UNROLL
        lax.fori_loop(0, whole, body, 0)
        group(whole, True)

    @pl.when(j == 0)
    def _():
        fetch(0, 0)

    @pl.when(j + 1 < n_tiles)
    def _():
        fetch(j + 1, 1 - slot)

    acc[...] = jnp.zeros_like(acc)
    for e in range(N_EXPERTS):
        win_copy(j, e, slot).wait()

    for e in range(N_EXPERTS):
        first, start = window(j, e)
        count = offs_ref[(j + 1) * N_EXPERTS + e] - offs_ref[j * N_EXPERTS + e]
        add_rows(ywin.at[slot, e], first, first - start, jnp.minimum(count, width))

        def overflow(k, c, first=first, count=count):
            begin = first + k * width
            start = jnp.minimum(begin, total - width)
            cp = pltpu.make_async_copy(y_hbm.at[pl.ds(start, width)], yover, osem)
            cp.start()
            cp.wait()
            add_rows(yover, begin, begin - start, jnp.minimum(count - k * width, width))
            return c
        lax.fori_loop(1, (count + width - 1) // width, overflow, 0)

    out = x1_ref[...] + acc[0:tile].reshape(tile, D_MODEL)
    o_ref[...] = _rms(out, norm_g[...]) if normalize else out


def _combine(x1, y, idx, offs, cap, tile, width, final_g):
    extra = () if final_g is None else (final_g,)
    n = x1.shape[0]
    n_tiles = n // tile
    grid_spec = pltpu.PrefetchScalarGridSpec(
        num_scalar_prefetch=2,
        grid=(n_tiles,),
        in_specs=[pl.BlockSpec((tile, D_MODEL), lambda i, offs, idx: (i, 0)),
                  pl.BlockSpec(memory_space=pl.ANY)]
                 + [pl.BlockSpec((1, D_MODEL), lambda i, offs, idx: (0, 0)) for _ in extra],
        out_specs=pl.BlockSpec((tile, D_MODEL), lambda i, offs, idx: (i, 0)),
        scratch_shapes=[pltpu.VMEM((tile + ROW_UNROLL, 1, D_MODEL), F32),
                        pltpu.VMEM((2, N_EXPERTS, width, 1, D_MODEL), F32),
                        pltpu.VMEM((width, 1, D_MODEL), F32),
                        pltpu.SemaphoreType.DMA((2,)), pltpu.SemaphoreType.DMA(())],
    )
    return pl.pallas_call(
        functools.partial(_combine_kernel, cap=cap, n_tiles=n_tiles, tile=tile, width=width,
                          normalize=final_g is not None),
        grid_spec=grid_spec,
        out_shape=jax.ShapeDtypeStruct((n, D_MODEL), F32),
        compiler_params=_params("arbitrary"),
        name="ec_combine",
    )(offs, idx, x1, y, *extra)


def _ec_ffn(x1, xn, aff_t, fw, final_g):
    n = x1.shape[0]
    cap = EC_CAPACITY_FACTOR * n // N_EXPERTS
    chunks = cap // SEL_BLOCK
    step_tokens = min(SEL_STEP, n)
    thr, need = _threshold(aff_t, cap)
    slots, offs_t = _compact(aff_t, thr, need, chunks, step_tokens)
    idx = slots[:N_EXPERTS * chunks, 0, :].reshape(-1).astype(jnp.int32)
    gate = slots[:N_EXPERTS * chunks, 1, :].reshape(-1, 1)
    offs = jnp.concatenate([offs_t[:, :, 0], jnp.full((1, N_EXPERTS), cap, jnp.int32)], axis=0).reshape(-1)
    y = _ffn(xn, idx, gate, fw, cap, min(FFN_ROWS, cap // 2))
    return _combine(x1, y, idx, offs, cap, step_tokens, min(COMBINE_W, cap), final_g)


def _pad_heads(w, n_heads, width):
    lead = w.shape[:-1]
    w = w.reshape(lead + (n_heads, width))
    w = jnp.pad(w, [(0, 0)] * len(lead) + [(0, 0), (0, LANE - width)])
    return w.reshape(lead + (n_heads * LANE,))


def _rope_tables(seq):
    rows = seq // GRID_W
    row = jnp.repeat(jnp.arange(rows, dtype=F32), GRID_W)
    col = jnp.tile(jnp.arange(GRID_W, dtype=F32), rows)

    def cos_sin(rot_dim):
        n_pair = rot_dim // 4
        freq = ROPE_THETA ** (-jnp.arange(n_pair, dtype=F32) / n_pair)
        ang = jnp.concatenate([row[:, None] * freq, col[:, None] * freq], axis=-1)
        return jnp.cos(ang), jnp.sin(ang)

    def place(parts):
        out = jnp.zeros((seq, LANE), F32)
        for start, val in parts:
            out = out.at[:, start:start + val.shape[1]].set(val)
        return out

    ca, sa = cos_sin(A_HEAD_DIM)
    cb, sb = cos_sin(B_ROPE)
    ha, hb = A_HEAD_DIM // 2, B_ROPE // 2
    ones = jnp.ones((seq, B_NOPE), F32)
    return (place([(0, ca), (ha, ca)]), place([(0, -sa)]), place([(ha, sa)]),
            place([(0, ones), (B_NOPE, cb), (B_NOPE + hb, cb)]), place([(B_NOPE, -sb)]),
            place([(B_NOPE + hb, sb)]))


def _prep_attn_layer(i, p, rope):
    w_in = p["w_in_attn"][i]
    o = 0
    w_qa = w_in[:, o:o + A_Q_W]; o += A_Q_W
    w_ka = w_in[:, o:o + A_KV_W]; o += A_KV_W
    w_va = w_in[:, o:o + A_KV_W]; o += A_KV_W
    w_cq = w_in[:, o:o + B_Q_RANK]; o += B_Q_RANK
    w_ckv = w_in[:, o:o + B_KV_RANK]; o += B_KV_RANK
    w_kr = w_in[:, o:o + B_ROPE]
    w_kr = jnp.pad(w_kr, [(0, 0), (B_NOPE, LANE - B_NOPE - B_ROPE)])
    w_all = jnp.concatenate([_pad_heads(w_qa, A_HEADS, A_HEAD_DIM), _pad_heads(w_ka, A_KV_HEADS, A_HEAD_DIM),
                             _pad_heads(w_va, A_KV_HEADS, A_HEAD_DIM), w_cq, w_ckv, w_kr], axis=1)
    w_ukv = p["w_ukv"][i].reshape(B_KV_RANK, B_HEADS, B_NOPE + B_V)
    w_uk = _pad_heads(w_ukv[:, :, :B_NOPE].reshape(B_KV_RANK, -1), B_HEADS, B_NOPE)
    w_uv = _pad_heads(w_ukv[:, :, B_NOPE:].reshape(B_KV_RANK, -1), B_HEADS, B_V)
    return {
        "g": p["attn_norm"][i][None, :],
        "w_all": w_all.astype(BF16),
        "gq": jnp.tile(jnp.pad(p["qk_norm_q"][i], (0, LANE - A_HEAD_DIM)), A_HEADS)[None, :],
        "gk": jnp.tile(jnp.pad(p["qk_norm_k"][i], (0, LANE - A_HEAD_DIM)), A_KV_HEADS)[None, :],
        "mq": p["mla_q_norm"][i][None, :],
        "wuq": _pad_heads(p["w_uq"][i], B_HEADS, B_QK).astype(BF16),
        "mkv": p["mla_kv_norm"][i][None, :],
        "wukv": jnp.concatenate([w_uk, w_uv], axis=1).astype(BF16),
        "w_out": p["w_out_attn"][i].astype(BF16),
        "rope": rope,
    }


def _prep_sgu_layer(i, p):
    b_s = p["b_spatial"][i]
    return {
        "g": p["sgu_norm"][i][None, :],
        "w_in": p["w_in_sgu"][i].astype(BF16),
        "ln_g": p["sgu_ln_g"][i][None, :],
        "ln_b": p["sgu_ln_b"][i][None, :],
        "w_s": p["w_spatial"][i].astype(BF16),
        "b_s": jnp.repeat(b_s.T, C_GROUP_W, axis=1),
        "w_out": p["w_out_sgu"][i].astype(BF16),
    }


def _prep_ffn_layer(l, p, w_gate, w_up, w_down):
    return {
        "g": p["ffn_norm"][l][None, :],
        "wrt": p["w_router"][l].T.astype(BF16),
        "layer": l, "w_gate": w_gate, "w_up": w_up, "w_down": w_down,
    }


def _pick(n, target):
    t = min(n, target)
    while n % t:
        t //= 2
    return t


def _trunk(x, attn_layers, sgu_layers, ffn_layers, final_g):
    batch, seq, _ = x.shape
    n = batch * seq
    x = x.reshape(n, D_MODEL)
    tile = _pick(seq, 512)
    tq = _pick(seq, 512)
    for l, fw in enumerate(ffn_layers):
        if l % 2 == 0:
            lw = attn_layers[l // 2]
            q, k, v = _attn_proj(x, lw, seq, tile)
            o = _attention(q, k, v, batch, seq, tq)
            x1, xn, aff_t = _attn_out(o, x, lw, fw, tile)
        else:
            x1, xn, aff_t = _sgu(x, sgu_layers[l // 2], fw, tile)
        last = l == len(ffn_layers) - 1
        x = _ec_ffn(x1, xn, aff_t, fw, final_g[None, :] if last else None)
    return x.reshape(batch, seq, D_MODEL)


def kernel(x_prompt, x_sample, attn_norm, w_in_attn, qk_norm_q, qk_norm_k, mla_q_norm, w_uq, mla_kv_norm, w_ukv, w_out_attn, sgu_norm, w_in_sgu, sgu_ln_g, sgu_ln_b, w_spatial, b_spatial, w_out_sgu, ffn_norm, w_router, w_gate, w_up, w_down, final_norm):
    p = dict(attn_norm=attn_norm, w_in_attn=w_in_attn, qk_norm_q=qk_norm_q, qk_norm_k=qk_norm_k,
             mla_q_norm=mla_q_norm, w_uq=w_uq, mla_kv_norm=mla_kv_norm, w_ukv=w_ukv,
             w_out_attn=w_out_attn, sgu_norm=sgu_norm, w_in_sgu=w_in_sgu, sgu_ln_g=sgu_ln_g,
             sgu_ln_b=sgu_ln_b, w_spatial=w_spatial, b_spatial=b_spatial, w_out_sgu=w_out_sgu,
             ffn_norm=ffn_norm, w_router=w_router)
    depth = ffn_norm.shape[0]
    wg, wu, wd = w_gate.astype(BF16), w_up.astype(BF16), w_down.astype(BF16)
    ffn_layers = [_prep_ffn_layer(l, p, wg, wu, wd) for l in range(depth)]
    sgu_layers = [_prep_sgu_layer(i, p) for i in range(sgu_norm.shape[0])]
    outs = []
    for x in (x_prompt, x_sample):
        rope = _rope_tables(x.shape[1])
        attn_layers = [_prep_attn_layer(i, p, rope) for i in range(attn_norm.shape[0])]
        outs.append(_trunk(x, attn_layers, sgu_layers, ffn_layers, final_norm))
    return tuple(outs)
```

```python
import functools

import jax
import jax.numpy as jnp
from jax import lax
from jax.experimental import pallas as pl
from jax.experimental.pallas import tpu as pltpu

D_MODEL = 1024
GRID_W = 64
ROPE_THETA = 10000.0
EPS = 1e-6

A_HEADS = 8
A_KV_HEADS = 2
A_GROUP = A_HEADS // A_KV_HEADS
A_HEAD_DIM = 64
A_Q_W = A_HEADS * A_HEAD_DIM
A_KV_W = A_KV_HEADS * A_HEAD_DIM

B_HEADS = 8
B_Q_RANK = 384
B_KV_RANK = 256
B_NOPE = 64
B_ROPE = 32
B_V = 64
B_QK = B_NOPE + B_ROPE

N_HEADS = A_HEADS + B_HEADS
HEADS_PER_STEP = 4
LOG2_E = 1.4426950408889634

C_WIDTH = D_MODEL
C_GROUPS = 8
C_GROUP_W = C_WIDTH // C_GROUPS
C_CHUNK = 128

N_EXPERTS = 16
EXPERT_FF = 2 * D_MODEL
EC_CAPACITY_FACTOR = 2

LANE = 128
VMEM_LIMIT = 56 * 1024 * 1024
SEL_BLOCK = LANE
SEL_STEP = 1024
FFN_ROWS = 512
COMBINE_W = 176
ROW_UNROLL = 8
BF16 = jnp.bfloat16
F32 = jnp.float32


def _params(*sem):
    return pltpu.CompilerParams(dimension_semantics=sem, vmem_limit_bytes=VMEM_LIMIT)


def _dot(a, b):
    return jnp.dot(a.astype(BF16), b.astype(BF16), preferred_element_type=F32)


def _dot_nt(a, b):
    return lax.dot_general(a.astype(BF16), b.astype(BF16), (((1,), (1,)), ((), ())),
                           preferred_element_type=F32)


def _rms(x, g):
    return x * lax.rsqrt(jnp.mean(x * x, axis=-1, keepdims=True) + EPS) * g


def _router_tail(x1, g_ref, wrt_ref, xn_ref, aff_ref):
    xn = _rms(x1, g_ref[...])
    xn_ref[...] = xn
    logits = _dot_nt(wrt_ref[...], xn)
    m = jnp.max(logits, axis=0, keepdims=True)
    e = jnp.exp(logits - m)
    aff_ref[...] = e / jnp.sum(e, axis=0, keepdims=True)


def _head_norm(z, n_heads, width):
    outs = []
    for h in range(n_heads):
        blk = z[:, h * LANE:(h + 1) * LANE]
        ms = jnp.sum(blk * blk, axis=-1, keepdims=True) * (1.0 / width)
        outs.append(blk * lax.rsqrt(ms + EPS))
    return jnp.concatenate(outs, axis=1)


def _rope(x, c, s_lo, s_hi, half, n_heads):
    w = x.shape[1]
    c = jnp.tile(c, (1, n_heads))
    s_lo = jnp.tile(s_lo, (1, n_heads))
    s_hi = jnp.tile(s_hi, (1, n_heads))
    return x * c + pltpu.roll(x, w - half, 1) * s_lo + pltpu.roll(x, half, 1) * s_hi


def _attn_proj_kernel(x_ref, g_ref, w_ref, gq_ref, gk_ref, mq_ref, wuq_ref, mkv_ref, wukv_ref,
                      ca_ref, sa1_ref, sa2_ref, cb_ref, sb1_ref, sb2_ref,
                      q_ref, k_ref, v_ref):
    h = _rms(x_ref[...], g_ref[...])
    z = _dot(h, w_ref[...])
    o = 0
    qa = z[:, o:o + A_HEADS * LANE]; o += A_HEADS * LANE
    ka = z[:, o:o + A_KV_HEADS * LANE]; o += A_KV_HEADS * LANE
    va = z[:, o:o + A_KV_HEADS * LANE]; o += A_KV_HEADS * LANE
    cq = z[:, o:o + B_Q_RANK]; o += B_Q_RANK
    ckv = z[:, o:o + B_KV_RANK]; o += B_KV_RANK
    kr = z[:, o:o + LANE]

    ca, sa1, sa2 = ca_ref[...], sa1_ref[...], sa2_ref[...]
    cb, sb1, sb2 = cb_ref[...], sb1_ref[...], sb2_ref[...]

    qa = _rope(_head_norm(qa, A_HEADS, A_HEAD_DIM) * gq_ref[...], ca, sa1, sa2, A_HEAD_DIM // 2, A_HEADS)
    ka = _rope(_head_norm(ka, A_KV_HEADS, A_HEAD_DIM) * gk_ref[...], ca, sa1, sa2, A_HEAD_DIM // 2,
               A_KV_HEADS)

    qb = _rope(_dot(_rms(cq, mq_ref[...]), wuq_ref[...]), cb, sb1, sb2, B_ROPE // 2, B_HEADS)
    kvb = _dot(_rms(ckv, mkv_ref[...]), wukv_ref[...])
    kr = _rope(kr, cb, sb1, sb2, B_ROPE // 2, 1)
    kb = kvb[:, :B_HEADS * LANE] + jnp.tile(kr, (1, B_HEADS))
    vb = kvb[:, B_HEADS * LANE:]

    q_ref[...] = jnp.concatenate([qa, qb], axis=1).astype(BF16)
    per_query_head = lambda x: [x[:, g * LANE:(g + 1) * LANE] for g in range(A_KV_HEADS)
                                for _ in range(A_GROUP)]
    k_ref[...] = jnp.concatenate(per_query_head(ka) + [kb], axis=1).astype(BF16)
    v = jnp.concatenate(per_query_head(va) + [vb], axis=1)
    lane = lax.broadcasted_iota(jnp.int32, v.shape, 1)
    v_ref[...] = jnp.where(lane % LANE == A_HEAD_DIM, 1.0, v).astype(BF16)


def _attn_proj(x, lw, seq, tile):
    n = x.shape[0]
    nt = n // tile
    per_seq = seq // tile
    full = lambda a: pl.BlockSpec(a.shape, lambda i: (0,) * a.ndim)
    tok = lambda w: pl.BlockSpec((tile, w), lambda i: (i, 0))
    tab = pl.BlockSpec((tile, LANE), lambda i: (i % per_seq, 0))
    weights = (lw["g"], lw["w_all"], lw["gq"], lw["gk"], lw["mq"], lw["wuq"], lw["mkv"], lw["wukv"])
    return pl.pallas_call(
        _attn_proj_kernel,
        grid=(nt,),
        in_specs=[tok(D_MODEL)] + [full(a) for a in weights] + [tab] * 6,
        out_specs=[tok(N_HEADS * LANE)] * 3,
        out_shape=[jax.ShapeDtypeStruct((n, N_HEADS * LANE), BF16)] * 3,
        compiler_params=_params("parallel"),
        name="attn_proj",
    )(x, *weights, *lw["rope"])


def _attn_kernel(q_ref, k_ref, v_ref, o_ref):
    pair = pl.program_id(1)
    scale = jnp.where(pair < A_HEADS // HEADS_PER_STEP, A_HEAD_DIM ** -0.5, B_QK ** -0.5).astype(F32)
    c = scale * LOG2_E
    heads = [slice(h * LANE, (h + 1) * LANE) for h in range(HEADS_PER_STEP)]
    scores = [_dot_nt(q_ref[:, sl], k_ref[:, sl]) for sl in heads]
    outs = []
    for s, sl in zip(scores, heads):
        m = jnp.max(s, axis=-1, keepdims=True)
        p = jnp.exp2((s - m) * c)
        o = _dot(p, v_ref[:, sl])
        outs.append((o / o[:, A_HEAD_DIM:A_HEAD_DIM + 1])[:, :A_HEAD_DIM])
    o_ref[...] = jnp.concatenate(outs, axis=1).astype(BF16)


def _attention(q, k, v, batch, seq, tq):
    n = q.shape[0]
    nq = seq // tq
    width = HEADS_PER_STEP * LANE
    return pl.pallas_call(
        _attn_kernel,
        grid=(batch, N_HEADS // HEADS_PER_STEP, nq),
        in_specs=[pl.BlockSpec((tq, width), lambda b, h, i: (b * nq + i, h)),
                  pl.BlockSpec((seq, width), lambda b, h, i: (b, h)),
                  pl.BlockSpec((seq, width), lambda b, h, i: (b, h))],
        out_specs=pl.BlockSpec((tq, HEADS_PER_STEP * A_HEAD_DIM), lambda b, h, i: (b * nq + i, h)),
        out_shape=jax.ShapeDtypeStruct((n, N_HEADS * A_HEAD_DIM), BF16),
        compiler_params=_params("parallel", "parallel", "parallel"),
        name="attention",
    )(q, k, v)


def _attn_out_kernel(o_ref, x_ref, w_ref, g_ref, wrt_ref, x1_ref, xn_ref, aff_ref):
    x1 = x_ref[...] + _dot(o_ref[...], w_ref[...])
    x1_ref[...] = x1
    _router_tail(x1, g_ref, wrt_ref, xn_ref, aff_ref)


def _tail_specs(n, tile):
    out_specs = [pl.BlockSpec((tile, D_MODEL), lambda i: (i, 0)),
                 pl.BlockSpec((tile, D_MODEL), lambda i: (i, 0)),
                 pl.BlockSpec((N_EXPERTS, tile), lambda i: (0, i))]
    out_shape = [jax.ShapeDtypeStruct((n, D_MODEL), F32),
                 jax.ShapeDtypeStruct((n, D_MODEL), F32),
                 jax.ShapeDtypeStruct((N_EXPERTS, n), F32)]
    return out_specs, out_shape


def _attn_out(o, x, lw, fw, tile):
    n = x.shape[0]
    full = lambda a: pl.BlockSpec(a.shape, lambda i: (0,) * a.ndim)
    out_specs, out_shape = _tail_specs(n, tile)
    weights = (lw["w_out"], fw["g"], fw["wrt"])
    return pl.pallas_call(
        _attn_out_kernel,
        grid=(n // tile,),
        in_specs=[pl.BlockSpec((tile, N_HEADS * A_HEAD_DIM), lambda i: (i, 0)),
                  pl.BlockSpec((tile, D_MODEL), lambda i: (i, 0))] + [full(a) for a in weights],
        out_specs=out_specs,
        out_shape=out_shape,
        compiler_params=_params("parallel"),
        name="attn_out",
    )(o, x, *weights)


def _sgu_kernel(x_ref, g_ref, win_ref, lng_ref, lnb_ref, ws_ref, bs_ref, wout_ref, gf_ref, wrt_ref,
                x1_ref, xn_ref, aff_ref):
    x = x_ref[...]
    tile = x.shape[0]
    z = jax.nn.gelu(_dot(_rms(x, g_ref[...]), win_ref[...]))
    u, v = z[:, :C_WIDTH], z[:, C_WIDTH:]
    mu = jnp.mean(v, axis=-1, keepdims=True)
    vc = v - mu
    v = vc * lax.rsqrt(jnp.mean(vc * vc, axis=-1, keepdims=True) + EPS) * lng_ref[...] + lnb_ref[...]
    v = v.astype(BF16)
    bias = bs_ref[...]
    rows = []
    for c in range(tile // C_CHUNK):
        vch = v[c * C_CHUNK:(c + 1) * C_CHUNK, :]
        cols = [_dot(ws_ref[g], vch[:, g * C_GROUP_W:(g + 1) * C_GROUP_W]) for g in range(C_GROUPS)]
        rows.append(jnp.concatenate(cols, axis=1) + bias)
    s = jnp.concatenate(rows, axis=0)
    x1 = x + _dot(u * s, wout_ref[...])
    x1_ref[...] = x1
    _router_tail(x1, gf_ref, wrt_ref, xn_ref, aff_ref)


def _sgu(x, lw, fw, tile):
    n = x.shape[0]
    full = lambda a: pl.BlockSpec(a.shape, lambda i: (0,) * a.ndim)
    out_specs, out_shape = _tail_specs(n, tile)
    weights = (lw["g"], lw["w_in"], lw["ln_g"], lw["ln_b"], lw["w_s"], lw["b_s"], lw["w_out"],
               fw["g"], fw["wrt"])
    return pl.pallas_call(
        _sgu_kernel,
        grid=(n // tile,),
        in_specs=[pl.BlockSpec((tile, D_MODEL), lambda i: (i, 0))] + [full(a) for a in weights],
        out_specs=out_specs,
        out_shape=out_shape,
        compiler_params=_params("parallel"),
        name="sgu",
    )(x, *weights)


def _threshold_kernel(aff_ref, thr_ref, need_ref, *, cap):
    def count(pred):
        return jnp.sum(jnp.where(pred, 1.0, 0.0), axis=1, keepdims=True)

    def body(i, cur):
        cand = cur | jnp.left_shift(jnp.int32(1), 30 - i)
        bits = lax.bitcast_convert_type(aff_ref[...], jnp.int32)
        return jnp.where(count(bits >= cand) >= cap, cand, cur)

    thr = lax.fori_loop(0, 31, body, jnp.zeros((N_EXPERTS, 1), jnp.int32))
    bits = lax.bitcast_convert_type(aff_ref[...], jnp.int32)
    thr_ref[...] = thr
    need_ref[...] = cap - count(bits > thr)


def _threshold(aff_t, cap):
    n = aff_t.shape[1]
    small = lambda dt: (pl.BlockSpec((N_EXPERTS, 1), lambda i: (0, 0)),
                        jax.ShapeDtypeStruct((N_EXPERTS, 1), dt))
    (s0, o0), (s1, o1) = small(jnp.int32), small(F32)
    return pl.pallas_call(
        functools.partial(_threshold_kernel, cap=cap),
        grid=(1,),
        in_specs=[pl.BlockSpec((N_EXPERTS, n), lambda i: (0, 0))],
        out_specs=[s0, s1],
        out_shape=[o0, o1],
        compiler_params=_params("arbitrary"),
        name="ec_threshold",
    )(aff_t)


def _compact_kernel(aff_ref, thr_ref, need_ref, slot_ref, offs_ref, carry_ref, tie_ref, *, chunks, step_tokens):
    step = pl.program_id(0)

    @pl.when(step == 0)
    def _():
        slot_ref[...] = jnp.zeros_like(slot_ref)
        carry_ref[...] = jnp.zeros_like(carry_ref)
        tie_ref[...] = jnp.zeros_like(tie_ref)

    offs_ref[0] = jnp.broadcast_to(carry_ref[...], (N_EXPERTS, LANE))

    row_i = lax.broadcasted_iota(jnp.int32, (SEL_BLOCK, SEL_BLOCK), 0)
    col_i = lax.broadcasted_iota(jnp.int32, (SEL_BLOCK, SEL_BLOCK), 1)
    upper = jnp.where(row_i < col_i, 1.0, 0.0).astype(BF16)
    lane = lax.broadcasted_iota(jnp.int32, (N_EXPERTS, SEL_BLOCK), 1)
    shifts = [1 << b for b in range(SEL_BLOCK.bit_length() - 1)]
    thr = thr_ref[...]
    need = need_ref[...]

    for sb in range(step_tokens // SEL_BLOCK):
        aff = aff_ref[:, sb * SEL_BLOCK:(sb + 1) * SEL_BLOCK]
        bits = lax.bitcast_convert_type(aff, jnp.int32)
        eq = bits == thr
        eq_f = jnp.where(eq, 1.0, 0.0)
        ties_before = jnp.dot(eq_f.astype(BF16), upper, preferred_element_type=F32) + tie_ref[...]
        sel = (bits > thr) | (eq & (ties_before < need))
        sel_f = jnp.where(sel, 1.0, 0.0)
        tie_ref[...] = tie_ref[...] + jnp.sum(eq_f, axis=1, keepdims=True)

        before = jnp.dot(sel_f.astype(BF16), upper, preferred_element_type=F32)
        count = jnp.sum(sel_f, axis=1, keepdims=True).astype(jnp.int32)
        carry = carry_ref[...]
        tok = (step * step_tokens + sb * SEL_BLOCK + lane).astype(F32)

        dist = jnp.where(sel, lane - before.astype(jnp.int32), 0)
        ids = jnp.where(sel, tok, 0.0)
        gates = jnp.where(sel, aff, 0.0)
        for sh in shifts:
            move = (dist & sh) != 0
            step_left = lambda x, zero: (jnp.where(move, zero, x)
                                         + pltpu.roll(jnp.where(move, x, zero), SEL_BLOCK - sh, 1))
            ids, gates, dist = step_left(ids, 0.0), step_left(gates, 0.0), step_left(dist, 0)

        off = carry & (SEL_BLOCK - 1)
        for sh in shifts:
            turn = (off & sh) != 0
            ids = jnp.where(turn, pltpu.roll(ids, sh, 1), ids)
            gates = jnp.where(turn, pltpu.roll(gates, sh, 1), gates)
        end = off + count
        in_lo = (lane >= off) & (lane < end)
        in_hi = lane < end - SEL_BLOCK
        pick = lambda m, x, e: jnp.where(m, x, 0.0)[e:e + 1, :]

        for e in range(N_EXPERTS):
            row = e * chunks + carry[e, 0] // SEL_BLOCK
            lo = jnp.concatenate([pick(in_lo, ids, e), pick(in_lo, gates, e)], axis=0)
            hi = jnp.concatenate([pick(in_hi, ids, e), pick(in_hi, gates, e)], axis=0)
            slot_ref[row] = slot_ref[row] + lo
            slot_ref[row + 1] = slot_ref[row + 1] + hi

        carry_ref[...] = carry + count


def _compact(aff_t, thr, need, chunks, step_tokens):
    n = aff_t.shape[1]
    n_steps = n // step_tokens
    n_rows = N_EXPERTS * chunks + 2
    small = lambda a: pl.BlockSpec(a.shape, lambda i: (0, 0))
    return pl.pallas_call(
        functools.partial(_compact_kernel, chunks=chunks, step_tokens=step_tokens),
        grid=(n_steps,),
        in_specs=[pl.BlockSpec((N_EXPERTS, step_tokens), lambda i: (0, i)), small(thr), small(need)],
        out_specs=[pl.BlockSpec((n_rows, 2, SEL_BLOCK), lambda i: (0, 0, 0)),
                   pl.BlockSpec((1, N_EXPERTS, LANE), lambda i: (i, 0, 0))],
        out_shape=[jax.ShapeDtypeStruct((n_rows, 2, SEL_BLOCK), F32),
                   jax.ShapeDtypeStruct((n_steps, N_EXPERTS, LANE), jnp.int32)],
        scratch_shapes=[pltpu.VMEM((N_EXPERTS, 1), jnp.int32), pltpu.VMEM((N_EXPERTS, 1), F32)],
        compiler_params=_params("arbitrary"),
        name="ec_compact",
    )(aff_t, thr, need)


def _ffn_kernel(idx_ref, gate_ref, xn_hbm, wg_ref, wu_ref, wd_ref, y_ref, buf0, buf1, sem, *, rows, n_pairs):
    pair = pl.program_id(0) * pl.num_programs(1) + pl.program_id(1)

    def row_copy(tok, buf, r, s):
        return pltpu.make_async_copy(xn_hbm.at[pl.ds(tok, 1), :], buf.at[pl.ds(r, 1), :], sem.at[s])

    def gather(chunk, buf, s):
        for r in range(rows):
            row_copy(idx_ref[chunk * rows + r], buf, r, s).start()

    def wait(buf, s):
        pltpu.make_async_copy(xn_hbm.at[pl.ds(0, rows), :], buf, sem.at[s]).wait()

    def compute(buf, half):
        xg = buf[...].astype(BF16)
        hid = jax.nn.silu(_dot(xg, wg_ref[0, 0])) * _dot(xg, wu_ref[0, 0])
        y = _dot(hid, wd_ref[0, 0]) * gate_ref[half * rows:(half + 1) * rows, :]
        y_ref[half * rows:(half + 1) * rows] = y.reshape(rows, 1, D_MODEL)

    @pl.when(pair == 0)
    def _():
        def body(r, c):
            row_copy(idx_ref[r], buf0, r, 0).start()
            return c
        lax.fori_loop(0, rows, body, 0)

    wait(buf0, 0)
    gather(2 * pair + 1, buf1, 1)
    compute(buf0, 0)
    wait(buf1, 1)
    gather(jnp.where(pair + 1 < n_pairs, 2 * pair + 2, 0), buf0, 0)
    compute(buf1, 1)

    @pl.when(pair == n_pairs - 1)
    def _():
        wait(buf0, 0)


def _ffn(xn, idx, gate, fw, cap, rows):
    pairs = cap // (2 * rows)
    layer = fw["layer"]
    w_spec = lambda shape: pl.BlockSpec((1, 1) + shape, lambda e, c, idx: (layer, e, 0, 0))
    grid_spec = pltpu.PrefetchScalarGridSpec(
        num_scalar_prefetch=1,
        grid=(N_EXPERTS, pairs),
        in_specs=[pl.BlockSpec((2 * rows, 1), lambda e, c, idx: (e * pairs + c, 0)),
                  pl.BlockSpec(memory_space=pl.ANY),
                  w_spec((D_MODEL, EXPERT_FF)), w_spec((D_MODEL, EXPERT_FF)), w_spec((EXPERT_FF, D_MODEL))],
        out_specs=pl.BlockSpec((2 * rows, 1, D_MODEL), lambda e, c, idx: (e * pairs + c, 0, 0)),
        scratch_shapes=[pltpu.VMEM((rows, D_MODEL), F32), pltpu.VMEM((rows, D_MODEL), F32),
                        pltpu.SemaphoreType.DMA((2,))],
    )
    return pl.pallas_call(
        functools.partial(_ffn_kernel, rows=rows, n_pairs=N_EXPERTS * pairs),
        grid_spec=grid_spec,
        out_shape=jax.ShapeDtypeStruct((N_EXPERTS * cap, 1, D_MODEL), F32),
        compiler_params=pltpu.CompilerParams(dimension_semantics=("arbitrary", "arbitrary"),
                                             vmem_limit_bytes=VMEM_LIMIT, disable_bounds_checks=True),
        name="ec_ffn",
    )(idx, gate, xn, fw["w_gate"], fw["w_up"], fw["w_down"])


def _combine_kernel(offs_ref, idx_ref, x1_ref, y_hbm, *rest, cap, n_tiles, tile, width, normalize):
    norm_g, (o_ref, acc, ywin, yover, sem, osem) = (rest[0], rest[1:]) if normalize else (None, rest)
    j = pl.program_id(0)
    slot = j % 2
    total = N_EXPERTS * cap

    def window(t, e):
        first = e * cap + offs_ref[t * N_EXPERTS + e]
        return first, jnp.minimum(first, total - width)

    def win_copy(t, e, s):
        return pltpu.make_async_copy(y_hbm.at[pl.ds(window(t, e)[1], width)], ywin.at[s, e], sem.at[s])

    def fetch(t, s):
        for e in range(N_EXPERTS):
            win_copy(t, e, s).start()

    def add_rows(win, first, shift, count):
        def group(g, partial):
            work = []
            for u in range(ROW_UNROLL):
                r = g * ROW_UNROLL + u
                if partial:
                    tok = idx_ref[jnp.minimum(first + r, total - 1)] - j * tile
                    dst = jnp.where(r < count, tok, tile + u)
                    src = jnp.minimum(shift + r, width - 1)
                else:
                    dst = idx_ref[first + r] - j * tile
                    src = shift + r
                work.append((dst, acc[dst] + win[src]))
            for dst, val in work:
                acc[dst] = val

        def body(g, c):
            group(g, False)
            return c
        whole = count // ROW_UNROLL
        lax.fori_loop(0, whole, body, 0)
        group(whole, True)

    @pl.when(j == 0)
    def _():
        fetch(0, 0)

    @pl.when(j + 1 < n_tiles)
    def _():
        fetch(j + 1, 1 - slot)

    acc[...] = jnp.zeros_like(acc)
    for e in range(N_EXPERTS):
        win_copy(j, e, slot).wait()

    for e in range(N_EXPERTS):
        first, start = window(j, e)
        count = offs_ref[(j + 1) * N_EXPERTS + e] - offs_ref[j * N_EXPERTS + e]
        add_rows(ywin.at[slot, e], first, first - start, jnp.minimum(count, width))

        def overflow(k, c, first=first, count=count):
            begin = first + k * width
            start = jnp.minimum(begin, total - width)
            cp = pltpu.make_async_copy(y_hbm.at[pl.ds(start, width)], yover, osem)
            cp.start()
            cp.wait()
            add_rows(yover, begin, begin - start, jnp.minimum(count - k * width, width))
            return c
        lax.fori_loop(1, (count + width - 1) // width, overflow, 0)

    out = x1_ref[...] + acc[0:tile].reshape(tile, D_MODEL)
    o_ref[...] = _rms(out, norm_g[...]) if normalize else out


def _combine(x1, y, idx, offs, cap, tile, width, final_g):
    extra = () if final_g is None else (final_g,)
    n = x1.shape[0]
    n_tiles = n // tile
    grid_spec = pltpu.PrefetchScalarGridSpec(
        num_scalar_prefetch=2,
        grid=(n_tiles,),
        in_specs=[pl.BlockSpec((tile, D_MODEL), lambda i, offs, idx: (i, 0)),
                  pl.BlockSpec(memory_space=pl.ANY)]
                 + [pl.BlockSpec((1, D_MODEL), lambda i, offs, idx: (0, 0)) for _ in extra],
        out_specs=pl.BlockSpec((tile, D_MODEL), lambda i, offs, idx: (i, 0)),
        scratch_shapes=[pltpu.VMEM((tile + ROW_UNROLL, 1, D_MODEL), F32),
                        pltpu.VMEM((2, N_EXPERTS, width, 1, D_MODEL), F32),
                        pltpu.VMEM((width, 1, D_MODEL), F32),
                        pltpu.SemaphoreType.DMA((2,)), pltpu.SemaphoreType.DMA(())],
    )
    return pl.pallas_call(
        functools.partial(_combine_kernel, cap=cap, n_tiles=n_tiles, tile=tile, width=width,
                          normalize=final_g is not None),
        grid_spec=grid_spec,
        out_shape=jax.ShapeDtypeStruct((n, D_MODEL), F32),
        compiler_params=_params("arbitrary"),
        name="ec_combine",
    )(offs, idx, x1, y, *extra)


def _ec_ffn(x1, xn, aff_t, fw, final_g):
    n = x1.shape[0]
    cap = EC_CAPACITY_FACTOR * n // N_EXPERTS
    chunks = cap // SEL_BLOCK
    step_tokens = min(SEL_STEP, n)
    thr, need = _threshold(aff_t, cap)
    slots, offs_t = _compact(aff_t, thr, need, chunks, step_tokens)
    idx = slots[:N_EXPERTS * chunks, 0, :].reshape(-1).astype(jnp.int32)
    gate = slots[:N_EXPERTS * chunks, 1, :].reshape(-1, 1)
    offs = jnp.concatenate([offs_t[:, :, 0], jnp.full((1, N_EXPERTS), cap, jnp.int32)], axis=0).reshape(-1)
    y = _ffn(xn, idx, gate, fw, cap, min(FFN_ROWS, cap // 2))
    return _combine(x1, y, idx, offs, cap, step_tokens, min(COMBINE_W, cap), final_g)


def _pad_heads(w, n_heads, width):
    lead = w.shape[:-1]
    w = w.reshape(lead + (n_heads, width))
    w = jnp.pad(w, [(0, 0)] * len(lead) + [(0, 0), (0, LANE - width)])
    return w.reshape(lead + (n_heads * LANE,))


def _rope_tables(seq):
    rows = seq // GRID_W
    row = jnp.repeat(jnp.arange(rows, dtype=F32), GRID_W)
    col = jnp.tile(jnp.arange(GRID_W, dtype=F32), rows)

    def cos_sin(rot_dim):
        n_pair = rot_dim // 4
        freq = ROPE_THETA ** (-jnp.arange(n_pair, dtype=F32) / n_pair)
        ang = jnp.concatenate([row[:, None] * freq, col[:, None] * freq], axis=-1)
        return jnp.cos(ang), jnp.sin(ang)

    def place(parts):
        out = jnp.zeros((seq, LANE), F32)
        for start, val in parts:
            out = out.at[:, start:start + val.shape[1]].set(val)
        return out

    ca, sa = cos_sin(A_HEAD_DIM)
    cb, sb = cos_sin(B_ROPE)
    ha, hb = A_HEAD_DIM // 2, B_ROPE // 2
    ones = jnp.ones((seq, B_NOPE), F32)
    return (place([(0, ca), (ha, ca)]), place([(0, -sa)]), place([(ha, sa)]),
            place([(0, ones), (B_NOPE, cb), (B_NOPE + hb, cb)]), place([(B_NOPE, -sb)]),
            place([(B_NOPE + hb, sb)]))


def _prep_attn_layer(i, p, rope):
    w_in = p["w_in_attn"][i]
    o = 0
    w_qa = w_in[:, o:o + A_Q_W]; o += A_Q_W
    w_ka = w_in[:, o:o + A_KV_W]; o += A_KV_W
    w_va = w_in[:, o:o + A_KV_W]; o += A_KV_W
    w_cq = w_in[:, o:o + B_Q_RANK]; o += B_Q_RANK
    w_ckv = w_in[:, o:o + B_KV_RANK]; o += B_KV_RANK
    w_kr = w_in[:, o:o + B_ROPE]
    w_kr = jnp.pad(w_kr, [(0, 0), (B_NOPE, LANE - B_NOPE - B_ROPE)])
    w_all = jnp.concatenate([_pad_heads(w_qa, A_HEADS, A_HEAD_DIM), _pad_heads(w_ka, A_KV_HEADS, A_HEAD_DIM),
                             _pad_heads(w_va, A_KV_HEADS, A_HEAD_DIM), w_cq, w_ckv, w_kr], axis=1)
    w_ukv = p["w_ukv"][i].reshape(B_KV_RANK, B_HEADS, B_NOPE + B_V)
    w_uk = _pad_heads(w_ukv[:, :, :B_NOPE].reshape(B_KV_RANK, -1), B_HEADS, B_NOPE)
    w_uv = _pad_heads(w_ukv[:, :, B_NOPE:].reshape(B_KV_RANK, -1), B_HEADS, B_V)
    return {
        "g": p["attn_norm"][i][None, :],
        "w_all": w_all.astype(BF16),
        "gq": jnp.tile(jnp.pad(p["qk_norm_q"][i], (0, LANE - A_HEAD_DIM)), A_HEADS)[None, :],
        "gk": jnp.tile(jnp.pad(p["qk_norm_k"][i], (0, LANE - A_HEAD_DIM)), A_KV_HEADS)[None, :],
        "mq": p["mla_q_norm"][i][None, :],
        "wuq": _pad_heads(p["w_uq"][i], B_HEADS, B_QK).astype(BF16),
        "mkv": p["mla_kv_norm"][i][None, :],
        "wukv": jnp.concatenate([w_uk, w_uv], axis=1).astype(BF16),
        "w_out": p["w_out_attn"][i].astype(BF16),
        "rope": rope,
    }


def _prep_sgu_layer(i, p):
    b_s = p["b_spatial"][i]
    return {
        "g": p["sgu_norm"][i][None, :],
        "w_in": p["w_in_sgu"][i].astype(BF16),
        "ln_g": p["sgu_ln_g"][i][None, :],
        "ln_b": p["sgu_ln_b"][i][None, :],
        "w_s": p["w_spatial"][i].astype(BF16),
        "b_s": jnp.repeat(b_s.T, C_GROUP_W, axis=1),
        "w_out": p["w_out_sgu"][i].astype(BF16),
    }


def _prep_ffn_layer(l, p, w_gate, w_up, w_down):
    return {
        "g": p["ffn_norm"][l][None, :],
        "wrt": p["w_router"][l].T.astype(BF16),
        "layer": l, "w_gate": w_gate, "w_up": w_up, "w_down": w_down,
    }


def _pick(n, target):
    t = min(n, target)
    while n % t:
        t //= 2
    return t


def _trunk(x, attn_layers, sgu_layers, ffn_layers, final_g):
    batch, seq, _ = x.shape
    n = batch * seq
    x = x.reshape(n, D_MODEL)
    tile = _pick(seq, 512)
    tq = _pick(seq, 1024)
    for l, fw in enumerate(ffn_layers):
        if l % 2 == 0:
            lw = attn_layers[l // 2]
            q, k, v = _attn_proj(x, lw, seq, tile)
            o = _attention(q, k, v, batch, seq, tq)
            x1, xn, aff_t = _attn_out(o, x, lw, fw, tile)
        else:
            x1, xn, aff_t = _sgu(x, sgu_layers[l // 2], fw, tile)
        last = l == len(ffn_layers) - 1
        x = _ec_ffn(x1, xn, aff_t, fw, final_g[None, :] if last else None)
    return x.reshape(batch, seq, D_MODEL)


def kernel(x_prompt, x_sample, attn_norm, w_in_attn, qk_norm_q, qk_norm_k, mla_q_norm, w_uq, mla_kv_norm, w_ukv, w_out_attn, sgu_norm, w_in_sgu, sgu_ln_g, sgu_ln_b, w_spatial, b_spatial, w_out_sgu, ffn_norm, w_router, w_gate, w_up, w_down, final_norm):
    p = dict(attn_norm=attn_norm, w_in_attn=w_in_attn, qk_norm_q=qk_norm_q, qk_norm_k=qk_norm_k,
             mla_q_norm=mla_q_norm, w_uq=w_uq, mla_kv_norm=mla_kv_norm, w_ukv=w_ukv,
             w_out_attn=w_out_attn, sgu_norm=sgu_norm, w_in_sgu=w_in_sgu, sgu_ln_g=sgu_ln_g,
             sgu_ln_b=sgu_ln_b, w_spatial=w_spatial, b_spatial=b_spatial, w_out_sgu=w_out_sgu,
             ffn_norm=ffn_norm, w_router=w_router)
    depth = ffn_norm.shape[0]
    wg, wu, wd = w_gate.astype(BF16), w_up.astype(BF16), w_down.astype(BF16)
    ffn_layers = [_prep_ffn_layer(l, p, wg, wu, wd) for l in range(depth)]
    sgu_layers = [_prep_sgu_layer(i, p) for i in range(sgu_norm.shape[0])]
    outs = []
    for x in (x_prompt, x_sample):
        rope = _rope_tables(x.shape[1])
        attn_layers = [_prep_attn_layer(i, p, rope) for i in range(attn_norm.shape[0])]
        outs.append(_trunk(x, attn_layers, sgu_layers, ffn_layers, final_norm))
    return tuple(outs)
```
